```python
import jax, jax.numpy as jnp
from jax import lax
import numpy as np

D_MODEL = 1024
BATCH = 4
SEQ = 8192
DEPTH = 4

D_MIX = D_MODEL
D_CONF = D_MIX // 4
D_ATT = D_MIX // 4
D_SC = D_MIX // 4
D_POOL = D_MIX - D_CONF - D_ATT - D_SC
HEAD_DIM = 64
N_ATT_HEADS = D_ATT // HEAD_DIM
CONF_KERNEL = 31
SC_KERNEL = 3
POOL_WINDOWS = (2, 4, 8, 16)
N_POOL_GROUPS = len(POOL_WINDOWS)
POOL_GROUP_DIM = D_POOL // N_POOL_GROUPS
D_FF = 4 * D_MODEL
D_PLE = 256
Q_BLOCK = 128
EPS = 1e-6
SPLITS = (2 * D_CONF, D_ATT, D_ATT, D_ATT, N_ATT_HEADS, D_SC, D_SC, D_SC, D_POOL)
D_IN = sum(SPLITS)
SPLIT_IDX = tuple(int(s) for s in np.cumsum(SPLITS)[:-1])

kernel_name = "hybrid_parallel_groups_fox_conv_pool"


def rms_norm(x, g):
    x32 = x.astype(jnp.float32)
    y = x32 * lax.rsqrt(jnp.mean(x32 * x32, axis=-1, keepdims=True) + EPS)
    return (y * g.astype(jnp.float32)).astype(x.dtype)


def layer_norm(x, g, b):
    x32 = x.astype(jnp.float32)
    mu = jnp.mean(x32, axis=-1, keepdims=True)
    xc = x32 - mu
    y = xc * lax.rsqrt(jnp.mean(xc * xc, axis=-1, keepdims=True) + EPS)
    return (y * g.astype(jnp.float32) + b.astype(jnp.float32)).astype(x.dtype)


def causal_depthwise_conv(u, w):
    k = w.shape[0]
    return lax.conv_general_dilated(
        u, w[:, None, :].astype(u.dtype), window_strides=(1,), padding=((k - 1, 0),),
        dimension_numbers=("NWC", "WIO", "NWC"), feature_group_count=u.shape[-1])


def conformer_conv(ab, w_dw, ln_g, ln_b, w_pw):
    a, b = jnp.split(ab, 2, axis=-1)
    u = a * jax.nn.sigmoid(b)
    u = causal_depthwise_conv(u, w_dw)
    u = jax.nn.silu(layer_norm(u, ln_g, ln_b))
    return u @ w_pw


def forgetting_attention(q, k, v, f_logit):
    b, s, _ = q.shape
    q = q.reshape(b, s, N_ATT_HEADS, HEAD_DIM).transpose(0, 2, 1, 3)
    k = k.reshape(b, s, N_ATT_HEADS, HEAD_DIM).transpose(0, 2, 1, 3)
    v = v.reshape(b, s, N_ATT_HEADS, HEAD_DIM).transpose(0, 2, 1, 3)
    log_f = jax.nn.log_sigmoid(f_logit.astype(jnp.float32))
    c = jnp.cumsum(log_f, axis=1).transpose(0, 2, 1)
    nb = s // Q_BLOCK
    qb = q.reshape(b, N_ATT_HEADS, nb, Q_BLOCK, HEAD_DIM).transpose(2, 0, 1, 3, 4)
    cb = c.reshape(b, N_ATT_HEADS, nb, Q_BLOCK).transpose(2, 0, 1, 3)
    pos = jnp.arange(s, dtype=jnp.int32)
    posb = pos.reshape(nb, Q_BLOCK)
    k32 = k.astype(jnp.float32)
    scale = HEAD_DIM ** -0.5

    def block(args):
        qi, ci, pi = args
        logits = jnp.einsum("bhqd,bhkd->bhqk", qi.astype(jnp.float32), k32) * scale
        logits = logits + ci[..., None] - c[:, :, None, :]
        mask = pi[:, None] >= pos[None, :]
        logits = jnp.where(mask, logits, -jnp.inf)
        probs = jax.nn.softmax(logits, axis=-1)
        return jnp.einsum("bhqk,bhkd->bhqd", probs.astype(v.dtype), v)

    o = lax.map(block, (qb, cb, posb))
    return o.transpose(1, 0, 3, 2, 4).reshape(b, s, D_ATT)


def short_conv_mixer(h, bg, cg, w_sc):
    return bg * causal_depthwise_conv(cg * h, w_sc)


def multiscale_pool(v, w_pool, scale):
    b, s, _ = v.shape
    v32 = v.astype(jnp.float32)
    count = jnp.arange(1, s + 1, dtype=jnp.float32)[None, :, None]
    groups = jnp.split(v32, N_POOL_GROUPS, axis=-1)
    outs = []
    for g, w in zip(groups, POOL_WINDOWS):
        csum = jnp.cumsum(g, axis=1)
        lag = jnp.pad(csum, ((0, 0), (w, 0), (0, 0)))[:, :s]
        mean = (csum - lag) / jnp.minimum(count, w)
        outs.append(mean - g)
    d = jnp.stack(outs, axis=2).astype(v.dtype)
    d = jnp.einsum("bsgc,gcd->bsgd", d, w_pool).reshape(b, s, D_POOL)
    return d * scale


def setup_inputs(seed: int = 0) -> dict:
    key = jax.random.key(seed)
    ks = jax.random.split(key, 24)
    f32 = jnp.float32
    L = DEPTH

    def nrm(k, shape, fan_in):
        return jax.random.normal(k, shape, f32) * (fan_in ** -0.5)

    def gain(k, shape):
        return 1.0 + 0.05 * jax.random.normal(k, shape, f32)

    return {
        "x": jax.random.normal(ks[0], (BATCH, SEQ, D_MODEL), f32),
        "p": jax.random.normal(ks[1], (DEPTH, BATCH, SEQ, D_PLE), f32),
        "g_mix_pre": gain(ks[2], (L, D_MODEL)),
        "w_in": nrm(ks[3], (L, D_MODEL, D_IN), D_MODEL),
        "b_forget": 0.1 * jax.random.normal(ks[4], (L, N_ATT_HEADS), f32),
        "w_conf_dw": nrm(ks[5], (L, CONF_KERNEL, D_CONF), CONF_KERNEL),
        "conf_ln_g": gain(ks[6], (L, D_CONF)),
        "conf_ln_b": 0.02 * jax.random.normal(ks[7], (L, D_CONF), f32),
        "w_conf_pw": nrm(ks[8], (L, D_CONF, D_CONF), D_CONF),
        "w_sc": nrm(ks[9], (L, SC_KERNEL, D_SC), SC_KERNEL),
        "w_pool": nrm(ks[10], (L, N_POOL_GROUPS, POOL_GROUP_DIM, POOL_GROUP_DIM), POOL_GROUP_DIM),
        "pool_scale": gain(ks[11], (L, D_POOL)),
        "w_out": nrm(ks[12], (L, D_MIX, D_MODEL), D_MIX),
        "g_mix_post": gain(ks[13], (L, D_MODEL)),
        "g_mlp_pre": gain(ks[14], (L, D_MODEL)),
        "w_up": nrm(ks[15], (L, D_MODEL, D_FF), D_MODEL),
        "w_down": nrm(ks[16], (L, D_FF, D_MODEL), D_FF),
        "g_mlp_post": gain(ks[17], (L, D_MODEL)),
        "g_ple_pre": gain(ks[18], (L, D_MODEL)),
        "w_ple_gate": nrm(ks[19], (L, D_MODEL, D_MODEL), D_MODEL),
        "w_ple_proj": nrm(ks[20], (L, D_PLE, D_MODEL), D_PLE),
        "g_ple_post": gain(ks[21], (L, D_MODEL)),
    }


def reference(x, p, g_mix_pre, w_in, b_forget, w_conf_dw, conf_ln_g, conf_ln_b, w_conf_pw,
              w_sc, w_pool, pool_scale, w_out, g_mix_post, g_mlp_pre, w_up, w_down,
              g_mlp_post, g_ple_pre, w_ple_gate, w_ple_proj, g_ple_post):
    h = x
    for i in range(DEPTH):
        xn = rms_norm(h, g_mix_pre[i])
        z = xn @ w_in[i]
        conf_ab, q, k, v, f_logit, sc_h, sc_b, sc_c, pool_v = jnp.split(z, SPLIT_IDX, axis=-1)
        y_conf = conformer_conv(conf_ab, w_conf_dw[i], conf_ln_g[i], conf_ln_b[i], w_conf_pw[i])
        y_att = forgetting_attention(q, k, v, f_logit + b_forget[i])
        y_sc = short_conv_mixer(sc_h, sc_b, sc_c, w_sc[i])
        y_pool = multiscale_pool(pool_v, w_pool[i], pool_scale[i])
        mix = jnp.concatenate([y_conf, y_att, y_sc, y_pool], axis=-1) @ w_out[i]
        h = h + rms_norm(mix, g_mix_post[i])
        hn = rms_norm(h, g_mlp_pre[i])
        ff = jnp.square(jax.nn.relu(hn @ w_up[i])) @ w_down[i]
        h = h + rms_norm(ff, g_mlp_post[i])
        gate = jax.nn.sigmoid(rms_norm(h, g_ple_pre[i]) @ w_ple_gate[i])
        e = (p[i] @ w_ple_proj[i]) * gate
        h = h + rms_norm(e, g_ple_post[i])
    return h
```

```python
import functools

import jax
import jax.numpy as jnp
import numpy as np
from jax import lax
from jax.experimental import pallas as pl
from jax.experimental.pallas import tpu as pltpu

F32 = jnp.float32
BF16 = jnp.bfloat16

D_MODEL = 1024
D_GRP = 256
HEAD_DIM = 64
N_HEADS = D_GRP // HEAD_DIM
CONF_KERNEL = 31
SC_KERNEL = 3
POOL_WINDOWS = (2, 4, 8, 16)
D_FF = 4 * D_MODEL
D_PLE = 256
EPS = 1e-6
LANES = 128
SUBLANES = 8

C_CONF = 0
C_Q = 512
C_K = 768
C_V = 1024
C_SCH = 1280
C_SCB = 1536
C_SCC = 1792
C_POOL = 2048
C_F = 2304
D_Z = C_F + LANES

HALO_U = 32
HALO_SC = 8
HALO_P = 16
ROW_CHUNK = 64

TM_FRONT = 512
TM_BACK = 512
TQ = 256
VMEM_LIMIT = 56 * 1024 * 1024


def _rms(x, g):
    return x * lax.rsqrt(jnp.mean(x * x, axis=-1, keepdims=True) + EPS) * g


def _sigmoid(x):
    return 1.0 / (1.0 + jnp.exp(-x))


def _dot(a, b):
    return jnp.dot(a, b, preferred_element_type=F32)


def _front_kernel(h_ref, g_ref, win_ref, bf_ref, wdw_ref, lng_ref, lnb_ref,
                  wpw_ref, wsc_ref, wpool_ref, pscale_ref,
                  q_ref, k_ref, v_ref, y3_ref, ccol_ref, crow_ref,
                  u_ext, ch_ext, pv_ext, conv_buf, c_carry, *, tm):
    s = pl.program_id(1)

    @pl.when(s == 0)
    def _():
        u_ext[0:HALO_U, :] = jnp.zeros((HALO_U, D_GRP), F32)
        ch_ext[0:HALO_SC, :] = jnp.zeros((HALO_SC, D_GRP), F32)
        pv_ext[0:HALO_P, :] = jnp.zeros((HALO_P, D_GRP), F32)
        c_carry[...] = jnp.zeros((1, LANES), F32)

    h = h_ref[0]
    xn = _rms(h, g_ref[...]).astype(BF16)
    z = _dot(xn, win_ref[...])

    q_ref[0] = z[:, C_Q:C_Q + D_GRP].astype(BF16)
    k_ref[0] = z[:, C_K:C_K + D_GRP].astype(BF16)
    v_ref[0] = z[:, C_V:C_V + D_GRP].astype(BF16)

    f = z[:, C_F:C_F + LANES] + bf_ref[...]
    logf = jnp.minimum(f, 0.0) - jnp.log(1.0 + jnp.exp(-jnp.abs(f)))
    row = lax.broadcasted_iota(jnp.int32, (tm, LANES), 0)
    c = logf
    shift = 1
    while shift < tm:
        c = c + jnp.where(row >= shift, pltpu.roll(c, shift, axis=0), 0.0)
        shift *= 2
    c = c + c_carry[...]
    c_carry[...] = c[tm - 1:tm, :]
    ccol_ref[0] = c
    crow_ref[0] = c.T[0:SUBLANES, :]

    u = z[:, C_CONF:C_CONF + D_GRP] * _sigmoid(z[:, C_CONF + D_GRP:C_CONF + 2 * D_GRP])
    u_ext[HALO_U:HALO_U + tm, :] = u
    for r0 in range(0, tm, ROW_CHUNK):
        acc = jnp.zeros((ROW_CHUNK, D_GRP), F32)
        for kk in range(CONF_KERNEL):
            off = HALO_U - (CONF_KERNEL - 1) + kk + r0
            acc = acc + wdw_ref[kk:kk + 1, :] * u_ext[off:off + ROW_CHUNK, :]
        conv_buf[r0:r0 + ROW_CHUNK, :] = acc
    u_ext[0:HALO_U, :] = u_ext[tm:tm + HALO_U, :]
    cv = conv_buf[...]
    mu = jnp.mean(cv, axis=-1, keepdims=True)
    xc = cv - mu
    ln = xc * lax.rsqrt(jnp.mean(xc * xc, axis=-1, keepdims=True) + EPS)
    ln = ln * lng_ref[...] + lnb_ref[...]
    act = ln * _sigmoid(ln)
    y3_ref[0, :, 0:D_GRP] = _dot(act.astype(BF16), wpw_ref[...]).astype(BF16)

    ch_ext[HALO_SC:HALO_SC + tm, :] = z[:, C_SCC:C_SCC + D_GRP] * z[:, C_SCH:C_SCH + D_GRP]
    sc = jnp.zeros((tm, D_GRP), F32)
    for kk in range(SC_KERNEL):
        off = HALO_SC - (SC_KERNEL - 1) + kk
        sc = sc + wsc_ref[kk:kk + 1, :] * ch_ext[off:off + tm, :]
    ch_ext[0:HALO_SC, :] = ch_ext[tm:tm + HALO_SC, :]
    y3_ref[0, :, D_GRP:2 * D_GRP] = (z[:, C_SCB:C_SCB + D_GRP] * sc).astype(BF16)

    pv = z[:, C_POOL:C_POOL + D_GRP]
    pv_ext[HALO_P:HALO_P + tm, :] = pv
    lane = lax.broadcasted_iota(jnp.int32, (1, D_GRP), 1)
    grp = lane // (D_GRP // len(POOL_WINDOWS))
    run = pv
    sel = jnp.zeros((tm, D_GRP), F32)
    lag = 1
    for gi, w in enumerate(POOL_WINDOWS):
        while lag < w:
            run = run + pv_ext[HALO_P - lag:HALO_P - lag + tm, :]
            lag += 1
        sel = jnp.where(grp == gi, run, sel)
    pv_ext[0:HALO_P, :] = pv_ext[tm:tm + HALO_P, :]
    wl = jnp.zeros((1, D_GRP), jnp.int32)
    for gi, w in enumerate(POOL_WINDOWS):
        wl = jnp.where(grp == gi, w, wl)
    pos = s * tm + lax.broadcasted_iota(jnp.int32, (tm, D_GRP), 0)
    cnt = jnp.minimum(pos + 1, wl).astype(F32)
    d = sel / cnt - pv
    yp = _dot(d.astype(BF16), wpool_ref[...]) * pscale_ref[...]
    y3_ref[0, :, 2 * D_GRP:3 * D_GRP] = yp.astype(BF16)


def _const_spec(shape):
    nd = len(shape)
    return pl.BlockSpec(shape, lambda *_: (0,) * nd)


def _front(h, g, win, bf, wdw, lng, lnb, wpw, wsc, wpool, pscale):
    B, S, _ = h.shape
    tm = min(TM_FRONT, S)
    grid = (B, S // tm)
    tok = lambda w: pl.BlockSpec((1, tm, w), lambda b, s: (b, s, 0))
    out_shape = (
        jax.ShapeDtypeStruct((B, S, D_GRP), BF16),
        jax.ShapeDtypeStruct((B, S, D_GRP), BF16),
        jax.ShapeDtypeStruct((B, S, D_GRP), BF16),
        jax.ShapeDtypeStruct((B, S, 3 * D_GRP), BF16),
        jax.ShapeDtypeStruct((B, S, LANES), F32),
        jax.ShapeDtypeStruct((B, SUBLANES, S), F32),
    )
    out_specs = (
        tok(D_GRP), tok(D_GRP), tok(D_GRP), tok(3 * D_GRP), tok(LANES),
        pl.BlockSpec((1, SUBLANES, tm), lambda b, s: (b, 0, s)),
    )
    in_specs = [tok(D_MODEL)] + [_const_spec(a.shape) for a in
                                 (g, win, bf, wdw, lng, lnb, wpw, wsc, wpool, pscale)]
    return pl.pallas_call(
        functools.partial(_front_kernel, tm=tm),
        grid=grid, in_specs=in_specs, out_specs=out_specs, out_shape=out_shape,
        scratch_shapes=[
            pltpu.VMEM((HALO_U + tm, D_GRP), F32),
            pltpu.VMEM((HALO_SC + tm, D_GRP), F32),
            pltpu.VMEM((HALO_P + tm, D_GRP), F32),
            pltpu.VMEM((tm, D_GRP), F32),
            pltpu.VMEM((1, LANES), F32),
        ],
        compiler_params=pltpu.CompilerParams(
            dimension_semantics=("arbitrary", "arbitrary"),
            vmem_limit_bytes=VMEM_LIMIT),
        name="front",
    )(h, g, win, bf, wdw, lng, lnb, wpw, wsc, wpool, pscale)


def _attn_kernel(q_ref, k_ref, v_ref, ccol_ref, crow_ref, o_ref, *, tq):
    pair = pl.program_id(1)
    qi = pl.program_id(2)
    q = q_ref[0]
    ccol = ccol_ref[0]
    lane = lax.broadcasted_iota(jnp.int32, (1, LANES), 1)
    rows = lax.broadcasted_iota(jnp.int32, (tq, tq), 0)
    cols = lax.broadcasted_iota(jnp.int32, (tq, tq), 1)
    causal = rows >= cols
    nt = (((1,), (1,)), ((), ()))

    outs = []
    for hh in range(2):
        head = pair * 2 + hh
        in_half = (lane // HEAD_DIM) == hh
        qh = jnp.where(in_half, q, jnp.zeros_like(q))
        cq = jnp.sum(jnp.where(lane == head, ccol, 0.0), axis=-1, keepdims=True)

        def block(j, carry, masked):
            m, l, acc = carry
            ks = pl.multiple_of(j * tq, tq)
            kb = k_ref[0, pl.ds(ks, tq), :]
            vb = v_ref[0, pl.ds(ks, tq), :]
            ck = crow_ref[0, pl.ds(head, 1), pl.ds(ks, tq)]
            sc = lax.dot_general(qh, kb, nt, preferred_element_type=F32)
            sc = sc + (cq - ck)
            if masked:
                sc = jnp.where(causal, sc, -jnp.inf)
            m_new = jnp.maximum(m, jnp.max(sc, axis=-1, keepdims=True))
            alpha = jnp.exp(m - m_new)
            p = jnp.exp(sc - m_new)
            l = alpha * l + jnp.sum(p, axis=-1, keepdims=True)
            acc = alpha * acc + _dot(p.astype(BF16), vb)
            return m_new, l, acc

        init = (jnp.full((tq, 1), -jnp.inf, F32), jnp.zeros((tq, 1), F32),
                jnp.zeros((tq, LANES), F32))
        carry = lax.fori_loop(0, qi, lambda j, cr: block(j, cr, False), init)
        m, l, acc = block(qi, carry, True)
        outs.append(acc / l)
    o_ref[0] = jnp.where((lane // HEAD_DIM) == 0, outs[0], outs[1]).astype(BF16)


def _attention(q, k, v, ccol, crow):
    B, S, _ = q.shape
    tq = min(TQ, S)
    grid = (B, D_GRP // LANES, S // tq)
    return pl.pallas_call(
        functools.partial(_attn_kernel, tq=tq),
        grid=grid,
        in_specs=[
            pl.BlockSpec((1, tq, LANES), lambda b, p, i: (b, i, p)),
            pl.BlockSpec((1, S, LANES), lambda b, p, i: (b, 0, p)),
            pl.BlockSpec((1, S, LANES), lambda b, p, i: (b, 0, p)),
            pl.BlockSpec((1, tq, LANES), lambda b, p, i: (b, i, 0)),
            pl.BlockSpec((1, SUBLANES, S), lambda b, p, i: (b, 0, 0)),
        ],
        out_specs=pl.BlockSpec((1, tq, LANES), lambda b, p, i: (b, i, p)),
        out_shape=jax.ShapeDtypeStruct((B, S, D_GRP), BF16),
        compiler_params=pltpu.CompilerParams(
            dimension_semantics=("arbitrary", "arbitrary", "arbitrary"),
            vmem_limit_bytes=VMEM_LIMIT),
        name="attn",
    )(q, k, v, ccol, crow)


def _back_kernel(h_ref, y3_ref, ya_ref, p_ref, gains_ref, wo3_ref, woa_ref,
                 wup_ref, wdn_ref, wgate_ref, wproj_ref, o_ref):
    h = h_ref[0]
    g_mix_post = gains_ref[0:1, :]
    g_mlp_pre = gains_ref[1:2, :]
    g_mlp_post = gains_ref[2:3, :]
    g_ple_pre = gains_ref[3:4, :]
    g_ple_post = gains_ref[4:5, :]

    mix = _dot(y3_ref[0], wo3_ref[...]) + _dot(ya_ref[0], woa_ref[...])
    h = h + _rms(mix, g_mix_post)

    hn = _rms(h, g_mlp_pre).astype(BF16)
    up = jnp.maximum(_dot(hn, wup_ref[...]), 0.0)
    ff = _dot((up * up).astype(BF16), wdn_ref[...])
    h = h + _rms(ff, g_mlp_post)

    gate = _sigmoid(_dot(_rms(h, g_ple_pre).astype(BF16), wgate_ref[...]))
    e = _dot(p_ref[0, 0].astype(BF16), wproj_ref[...]) * gate
    o_ref[0] = h + _rms(e, g_ple_post)


def _back(h, y3, ya, p, layer, gains, wo3, woa, wup, wdn, wgate, wproj):
    B, S, _ = h.shape
    tm = min(TM_BACK, S)
    grid = (B, S // tm)
    tok = lambda w: pl.BlockSpec((1, tm, w), lambda b, s: (b, s, 0))
    wspec = lambda a: pl.BlockSpec(a.shape, lambda b, s: (0,) * a.ndim,
                                   pipeline_mode=pl.Buffered(1))
    in_specs = [
        tok(D_MODEL), tok(3 * D_GRP), tok(D_GRP),
        pl.BlockSpec((1, 1, tm, D_PLE), lambda b, s: (layer, b, s, 0)),
    ] + [wspec(a) for a in (gains, wo3, woa, wup, wdn, wgate, wproj)]
    return pl.pallas_call(
        _back_kernel,
        grid=grid, in_specs=in_specs, out_specs=tok(D_MODEL),
        out_shape=jax.ShapeDtypeStruct((B, S, D_MODEL), F32),
        compiler_params=pltpu.CompilerParams(
            dimension_semantics=("arbitrary", "arbitrary"),
            vmem_limit_bytes=VMEM_LIMIT),
        name="back",
    )(h, y3, ya, p, gains, wo3, woa, wup, wdn, wgate, wproj)


def _permute_w_in(w_in):
    o = 0
    conf = w_in[..., o:o + 512]; o += 512
    q = w_in[..., o:o + 256] * (HEAD_DIM ** -0.5); o += 256
    k = w_in[..., o:o + 256]; o += 256
    v = w_in[..., o:o + 256]; o += 256
    f = w_in[..., o:o + N_HEADS]; o += N_HEADS
    rest = w_in[..., o:]
    f = jnp.pad(f, ((0, 0), (0, 0), (0, LANES - N_HEADS)))
    return jnp.concatenate([conf, q, k, v, rest, f], axis=-1).astype(BF16)


def _block_diag(w_pool):
    L, G, C, _ = w_pool.shape
    eye = jnp.eye(G, dtype=w_pool.dtype)
    return jnp.einsum("lgcd,gh->lgchd", w_pool, eye).reshape(L, G * C, G * C)


def kernel(x, p, g_mix_pre, w_in, b_forget, w_conf_dw, conf_ln_g, conf_ln_b, w_conf_pw, w_sc, w_pool, pool_scale, w_out, g_mix_post, g_mlp_pre, w_up, w_down, g_mlp_post, g_ple_pre, w_ple_gate, w_ple_proj, g_ple_post):
    L = w_in.shape[0]
    win = _permute_w_in(w_in)
    bf = jnp.pad(b_forget, ((0, 0), (0, LANES - N_HEADS)))[:, None, :]
    wpw = w_conf_pw.astype(BF16)
    wpool = _block_diag(w_pool).astype(BF16)
    wo3 = jnp.concatenate([w_out[:, 0:256], w_out[:, 512:1024]], axis=1).astype(BF16)
    woa = w_out[:, 256:512].astype(BF16)
    wup = w_up.astype(BF16)
    wdn = w_down.astype(BF16)
    wgate = w_ple_gate.astype(BF16)
    wproj = w_ple_proj.astype(BF16)
    zero = jnp.zeros_like(g_mix_post)
    gains = jnp.stack([g_mix_post, g_mlp_pre, g_mlp_post, g_ple_pre, g_ple_post,
                       zero, zero, zero], axis=1)

    h = x
    for i in range(L):
        q, k, v, y3, ccol, crow = _front(
            h, g_mix_pre[i][None, :], win[i], bf[i], w_conf_dw[i],
            conf_ln_g[i][None, :], conf_ln_b[i][None, :], wpw[i], w_sc[i],
            wpool[i], pool_scale[i][None, :])
        ya = _attention(q, k, v, ccol, crow)
        h = _back(h, y3, ya, p, i, gains[i], wo3[i], woa[i], wup[i], wdn[i],
                  wgate[i], wproj[i])
    return h
```

```python
import functools

import jax
import jax.numpy as jnp
import numpy as np
from jax import lax
from jax.experimental import pallas as pl
from jax.experimental.pallas import tpu as pltpu

F32 = jnp.float32
BF16 = jnp.bfloat16

D_MODEL = 1024
D_GRP = 256
HEAD_DIM = 64
N_HEADS = D_GRP // HEAD_DIM
CONF_KERNEL = 31
SC_KERNEL = 3
POOL_WINDOWS = (2, 4, 8, 16)
D_FF = 4 * D_MODEL
D_PLE = 256
EPS = 1e-6
LANES = 128
SUBLANES = 8
LOG2E = 1.4426950408889634

C_CONF = 0
C_Q = 512
C_K = 768
C_V = 1024
C_SCH = 1280
C_SCB = 1536
C_SCC = 1792
C_POOL = 2048
C_F = 2304
D_Z = C_F + LANES

HALO_U = 32
HALO_SC = 8
HALO_P = 16
ROW_CHUNK = 64
STAT_ROWS = 128

TM_FRONT = 512
TM_BACK = 512
TQ = 256
VMEM_LIMIT = 56 * 1024 * 1024

SKIP_LOG2 = 104.0 * LOG2E + 1.0
NORM_SLACK = 1.02


def _rms(x, g):
    return x * lax.rsqrt(jnp.mean(x * x, axis=-1, keepdims=True) + EPS) * g


def _sigmoid(x):
    return 1.0 / (1.0 + jnp.exp(-x))


def _dot(a, b):
    return jnp.dot(a, b, preferred_element_type=F32)


def _front_kernel(h_ref, g_ref, win_ref, bf_ref, wdw_ref, lng_ref, lnb_ref,
                  wpw_ref, wsc_ref, wpool_ref, pscale_ref, e2_ref,
                  q_ref, k_ref, vt_ref, y3_ref, ccol_ref, crow_ref, bstat_ref,
                  u_ext, ch_ext, pv_ext, conv_buf, sc_buf, pool_buf, c_carry, *, tm):
    s = pl.program_id(1)

    @pl.when(s == 0)
    def _():
        u_ext[0:HALO_U, :] = jnp.zeros((HALO_U, D_GRP), F32)
        ch_ext[0:HALO_SC, :] = jnp.zeros((HALO_SC, D_GRP), F32)
        pv_ext[0:HALO_P, :] = jnp.zeros((HALO_P, D_GRP), F32)
        c_carry[...] = jnp.zeros((1, LANES), F32)

    h = h_ref[0]
    xn = _rms(h, g_ref[...]).astype(BF16)
    z = _dot(xn, win_ref[...])

    qb = z[:, C_Q:C_Q + D_GRP].astype(BF16)
    kb = z[:, C_K:C_K + D_GRP].astype(BF16)
    q_ref[0] = qb
    k_ref[0] = kb
    vt_ref[0] = z[:, C_V:C_V + D_GRP].T.astype(BF16)

    qf = qb.astype(F32)
    kf = kb.astype(F32)
    sq = jnp.concatenate([qf * qf, kf * kf], axis=1).astype(BF16)
    n2 = _dot(sq, e2_ref[...])

    f = z[:, C_F:C_F + LANES] + bf_ref[...]
    logf = (jnp.minimum(f, 0.0) - jnp.log(1.0 + jnp.exp(-jnp.abs(f)))) * LOG2E
    row = lax.broadcasted_iota(jnp.int32, (tm, LANES), 0)
    c = logf
    shift = 1
    while shift < tm:
        c = c + jnp.where(row >= shift, pltpu.roll(c, shift, axis=0), 0.0)
        shift *= 2
    c = c + c_carry[...]
    c_carry[...] = c[tm - 1:tm, :]
    ccol_ref[0] = c
    crow_ref[0] = c.T[0:SUBLANES, :]

    nb = tm // STAT_ROWS
    c3 = c.reshape(nb, STAT_ROWS, LANES)
    bstat_ref[0, 0, 0:nb, :] = jnp.max(n2.reshape(nb, STAT_ROWS, LANES), axis=1)
    bstat_ref[0, 0, nb:2 * nb, :] = c3[:, 0, :]
    bstat_ref[0, 0, 2 * nb:3 * nb, :] = c3[:, STAT_ROWS - 1, :]

    u = z[:, C_CONF:C_CONF + D_GRP] * _sigmoid(z[:, C_CONF + D_GRP:C_CONF + 2 * D_GRP])
    u_ext[HALO_U:HALO_U + tm, :] = u
    n_a = (CONF_KERNEL - 1) // SUBLANES + 1
    rc = ROW_CHUNK
    for r0 in range(0, tm, rc):
        base = HALO_U + r0 - SUBLANES
        ua = [u_ext[base - SUBLANES * a:base - SUBLANES * a + rc + SUBLANES, :]
              for a in range(n_a)]
        total = None
        for r in range(SUBLANES):
            part = None
            for a in range(n_a):
                d = SUBLANES * a + r
                if d > CONF_KERNEL - 1:
                    continue
                kk = CONF_KERNEL - 1 - d
                term = wdw_ref[kk:kk + 1, :] * ua[a]
                part = term if part is None else part + term
            if r:
                part = pltpu.roll(part, r, axis=0)
            part = part[SUBLANES:, :]
            total = part if total is None else total + part
        conv_buf[r0:r0 + rc, :] = total
    u_ext[0:HALO_U, :] = u_ext[tm:tm + HALO_U, :]
    cv = conv_buf[...]
    mu = jnp.mean(cv, axis=-1, keepdims=True)
    xc = cv - mu
    ln = xc * lax.rsqrt(jnp.mean(xc * xc, axis=-1, keepdims=True) + EPS)
    ln = ln * lng_ref[...] + lnb_ref[...]
    act = ln * _sigmoid(ln)
    y3_ref[0, :, 0:D_GRP] = _dot(act.astype(BF16), wpw_ref[...]).astype(BF16)

    ch_ext[HALO_SC:HALO_SC + tm, :] = z[:, C_SCC:C_SCC + D_GRP] * z[:, C_SCH:C_SCH + D_GRP]
    for r0 in range(0, tm, rc):
        xs = ch_ext[r0:r0 + rc + HALO_SC, :]
        acc = wsc_ref[SC_KERNEL - 1:SC_KERNEL, :] * xs
        for d in range(1, SC_KERNEL):
            kk = SC_KERNEL - 1 - d
            acc = acc + wsc_ref[kk:kk + 1, :] * pltpu.roll(xs, d, axis=0)
        sc_buf[r0:r0 + rc, :] = acc[HALO_SC:, :]
    ch_ext[0:HALO_SC, :] = ch_ext[tm:tm + HALO_SC, :]
    y3_ref[0, :, D_GRP:2 * D_GRP] = (z[:, C_SCB:C_SCB + D_GRP] * sc_buf[...]).astype(BF16)

    pv = z[:, C_POOL:C_POOL + D_GRP]
    pv_ext[HALO_P:HALO_P + tm, :] = pv
    lane = lax.broadcasted_iota(jnp.int32, (1, D_GRP), 1)
    grp = lane // (D_GRP // len(POOL_WINDOWS))
    for r0 in range(0, tm, rc):
        run = pv_ext[r0:r0 + rc + HALO_P, :]
        sel = None
        w = 1
        for gi, wg in enumerate(POOL_WINDOWS):
            while w < wg:
                run = run + pltpu.roll(run, w, axis=0)
                w *= 2
            sel = run if sel is None else jnp.where(grp >= gi, run, sel)
        pool_buf[r0:r0 + rc, :] = sel[HALO_P:, :]
    pv_ext[0:HALO_P, :] = pv_ext[tm:tm + HALO_P, :]
    wl = jnp.zeros((1, D_GRP), jnp.int32)
    for gi, wg in enumerate(POOL_WINDOWS):
        wl = jnp.where(grp == gi, wg, wl)
    pos = s * tm + lax.broadcasted_iota(jnp.int32, (tm, D_GRP), 0)
    cnt = jnp.minimum(pos + 1, wl).astype(F32)
    d = pool_buf[...] / cnt - pv
    yp = _dot(d.astype(BF16), wpool_ref[...]) * pscale_ref[...]
    y3_ref[0, :, 2 * D_GRP:3 * D_GRP] = yp.astype(BF16)


def _const_spec(shape):
    nd = len(shape)
    return pl.BlockSpec(shape, lambda *_: (0,) * nd)


def _front(h, g, win, bf, wdw, lng, lnb, wpw, wsc, wpool, pscale, e2):
    B, S, _ = h.shape
    tm = min(TM_FRONT, S)
    nb = tm // STAT_ROWS
    grid = (B, S // tm)
    tok = lambda w: pl.BlockSpec((1, tm, w), lambda b, s: (b, s, 0))
    out_shape = (
        jax.ShapeDtypeStruct((B, S, D_GRP), BF16),
        jax.ShapeDtypeStruct((B, S, D_GRP), BF16),
        jax.ShapeDtypeStruct((B, D_GRP, S), BF16),
        jax.ShapeDtypeStruct((B, S, 3 * D_GRP), BF16),
        jax.ShapeDtypeStruct((B, S, LANES), F32),
        jax.ShapeDtypeStruct((B, SUBLANES, S), F32),
        jax.ShapeDtypeStruct((B, S // tm, 3 * nb, LANES), F32),
    )
    out_specs = (
        tok(D_GRP), tok(D_GRP),
        pl.BlockSpec((1, D_GRP, tm), lambda b, s: (b, 0, s)),
        tok(3 * D_GRP), tok(LANES),
        pl.BlockSpec((1, SUBLANES, tm), lambda b, s: (b, 0, s)),
        pl.BlockSpec((1, 1, 3 * nb, LANES), lambda b, s: (b, s, 0, 0)),
    )
    consts = (g, win, bf, wdw, lng, lnb, wpw, wsc, wpool, pscale, e2)
    in_specs = [tok(D_MODEL)] + [_const_spec(a.shape) for a in consts]
    return pl.pallas_call(
        functools.partial(_front_kernel, tm=tm),
        grid=grid, in_specs=in_specs, out_specs=out_specs, out_shape=out_shape,
        scratch_shapes=[
            pltpu.VMEM((HALO_U + tm, D_GRP), F32),
            pltpu.VMEM((HALO_SC + tm, D_GRP), F32),
            pltpu.VMEM((HALO_P + tm, D_GRP), F32),
            pltpu.VMEM((tm, D_GRP), F32),
            pltpu.VMEM((tm, D_GRP), F32),
            pltpu.VMEM((tm, D_GRP), F32),
            pltpu.VMEM((1, LANES), F32),
        ],
        compiler_params=pltpu.CompilerParams(
            dimension_semantics=("arbitrary", "arbitrary"),
            vmem_limit_bytes=VMEM_LIMIT),
        name="front",
    )(h, *consts)


def _first_needed_block(bstat, tq):
    B, nt, nb3, _ = bstat.shape
    nb = nb3 // 3
    flat = lambda a: a.reshape(B, nt * nb, LANES)
    n2 = flat(bstat[:, :, 0:nb])
    cs = flat(bstat[:, :, nb:2 * nb])[..., 0:N_HEADS]
    ce = flat(bstat[:, :, 2 * nb:3 * nb])[..., 0:N_HEADS]
    r = tq // STAT_ROWS
    nq = nt * nb // r
    qn = jnp.sqrt(jnp.max(n2[..., 0:N_HEADS].reshape(B, nq, r, N_HEADS), axis=2))
    kn = jnp.sqrt(jnp.max(n2[..., N_HEADS:2 * N_HEADS], axis=1, keepdims=True))
    bound = 2.0 * NORM_SLACK * qn * kn + 1.0
    c_start = cs[:, ::r]
    c_end = ce[:, r - 1::r]
    gap = c_start[:, :, None, :] - c_end[:, None, :, :]
    ii = jnp.arange(nq)[:, None]
    jj = jnp.arange(nq)[None, :]
    skip = (gap + bound[:, :, None, :] <= -SKIP_LOG2) & (jj < ii)[None, :, :, None]
    lead = jnp.cumprod(skip.astype(jnp.int32), axis=2)
    per_head = jnp.sum(lead, axis=2)
    return jnp.min(per_head, axis=2).astype(jnp.int32)


def _attn_kernel(jlo_ref, q_ref, k_ref, vt_ref, ccol_ref, crow_ref, o_ref, *, tq):
    b = pl.program_id(0)
    qi = pl.program_id(1)
    qs = pl.multiple_of(qi * tq, tq)
    key_i = lax.broadcasted_iota(jnp.int32, (tq, tq), 0)
    qry_i = lax.broadcasted_iota(jnp.int32, (tq, tq), 1)
    causal_t = key_i <= qry_i
    lane = lax.broadcasted_iota(jnp.int32, (1, LANES), 1)
    nt = (((1,), (1,)), ((), ()))
    hpp = LANES // HEAD_DIM

    q_heads, c_refs = [], []
    for head in range(N_HEADS):
        lo = (head // hpp) * LANES
        q_pair = q_ref[0, :, lo:lo + LANES]
        q_heads.append(jnp.where((lane // HEAD_DIM) == head % hpp, q_pair,
                                 jnp.zeros_like(q_pair)))
        c_refs.append(crow_ref[0, head:head + 1, pl.ds(qs, LANES)][:, 0:1])

    def block(j, carry, masked):
        ks = pl.multiple_of(j * tq, tq)
        new = []
        for head in range(N_HEADS):
            m, l, acc = carry[head]
            lo = (head // hpp) * LANES
            fo = head * HEAD_DIM
            kb = k_ref[0, pl.ds(ks, tq), lo:lo + LANES]
            vtb = vt_ref[0, fo:fo + HEAD_DIM, pl.ds(ks, tq)]
            ck = ccol_ref[0, pl.ds(ks, tq), head:head + 1] - c_refs[head]
            st = lax.dot_general(kb, q_heads[head], nt, preferred_element_type=F32)
            st = st - ck
            if masked:
                st = jnp.where(causal_t, st, -jnp.inf)
            m_new = jnp.maximum(m, jnp.max(st, axis=0, keepdims=True))
            alpha = jnp.exp2(m - m_new)
            p = jnp.exp2(st - m_new)
            l = alpha * l + jnp.sum(p, axis=0, keepdims=True)
            acc = alpha * acc + _dot(vtb, p.astype(BF16))
            new.append((m_new, l, acc))
        return tuple(new)

    init = tuple((jnp.full((1, tq), -jnp.inf, F32), jnp.zeros((1, tq), F32),
                  jnp.zeros((HEAD_DIM, tq), F32)) for _ in range(N_HEADS))
    carry = lax.fori_loop(jlo_ref[b, qi], qi, lambda j, cr: block(j, cr, False), init)
    final = block(qi, carry, True)
    o_t = jnp.concatenate([acc / l for _, l, acc in final], axis=0)
    o_ref[0] = o_t.T.astype(BF16)


def _attention(jlo, q, k, vt, ccol, crow):
    B, S, _ = q.shape
    tq = min(TQ, S)
    grid_spec = pltpu.PrefetchScalarGridSpec(
        num_scalar_prefetch=1,
        grid=(B, S // tq),
        in_specs=[
            pl.BlockSpec((1, tq, D_GRP), lambda b, i, jlo: (b, i, 0)),
            pl.BlockSpec((1, S, D_GRP), lambda b, i, jlo: (b, 0, 0)),
            pl.BlockSpec((1, D_GRP, S), lambda b, i, jlo: (b, 0, 0)),
            pl.BlockSpec((1, S, LANES), lambda b, i, jlo: (b, 0, 0)),
            pl.BlockSpec((1, SUBLANES, S), lambda b, i, jlo: (b, 0, 0)),
        ],
        out_specs=pl.BlockSpec((1, tq, D_GRP), lambda b, i, jlo: (b, i, 0)),
    )
    return pl.pallas_call(
        functools.partial(_attn_kernel, tq=tq),
        grid_spec=grid_spec,
        out_shape=jax.ShapeDtypeStruct((B, S, D_GRP), BF16),
        compiler_params=pltpu.CompilerParams(
            dimension_semantics=("arbitrary", "arbitrary"),
            vmem_limit_bytes=VMEM_LIMIT),
        name="attn",
    )(jlo, q, k, vt, ccol, crow)


def _back_kernel(h_ref, y3_ref, ya_ref, p_ref, gains_ref, wo3_ref, woa_ref,
                 wup_ref, wdn_ref, wgate_ref, wproj_ref, o_ref):
    h = h_ref[0]
    g_mix_post = gains_ref[0:1, :]
    g_mlp_pre = gains_ref[1:2, :]
    g_mlp_post = gains_ref[2:3, :]
    g_ple_pre = gains_ref[3:4, :]
    g_ple_post = gains_ref[4:5, :]

    mix = _dot(y3_ref[0], wo3_ref[...]) + _dot(ya_ref[0], woa_ref[...])
    h = h + _rms(mix, g_mix_post)

    hn = _rms(h, g_mlp_pre).astype(BF16)
    up = jnp.maximum(_dot(hn, wup_ref[...]), 0.0)
    ff = _dot((up * up).astype(BF16), wdn_ref[...])
    h = h + _rms(ff, g_mlp_post)

    gate = _sigmoid(_dot(_rms(h, g_ple_pre).astype(BF16), wgate_ref[...]))
    e = _dot(p_ref[0, 0].astype(BF16), wproj_ref[...]) * gate
    o_ref[0] = h + _rms(e, g_ple_post)


def _back(h, y3, ya, p, layer, gains, wo3, woa, wup, wdn, wgate, wproj):
    B, S, _ = h.shape
    tm = min(TM_BACK, S)
    grid = (B, S // tm)
    tok = lambda w: pl.BlockSpec((1, tm, w), lambda b, s: (b, s, 0))
    wspec = lambda a: pl.BlockSpec(a.shape, lambda b, s: (0,) * a.ndim,
                                   pipeline_mode=pl.Buffered(1))
    in_specs = [
        tok(D_MODEL), tok(3 * D_GRP), tok(D_GRP),
        pl.BlockSpec((1, 1, tm, D_PLE), lambda b, s: (layer, b, s, 0)),
    ] + [wspec(a) for a in (gains, wo3, woa, wup, wdn, wgate, wproj)]
    return pl.pallas_call(
        _back_kernel,
        grid=grid, in_specs=in_specs, out_specs=tok(D_MODEL),
        out_shape=jax.ShapeDtypeStruct((B, S, D_MODEL), F32),
        compiler_params=pltpu.CompilerParams(
            dimension_semantics=("arbitrary", "arbitrary"),
            vmem_limit_bytes=VMEM_LIMIT),
        name="back",
    )(h, y3, ya, p, gains, wo3, woa, wup, wdn, wgate, wproj)


def _permute_w_in(w_in):
    o = 0
    conf = w_in[..., o:o + 512]; o += 512
    q = w_in[..., o:o + 256] * (HEAD_DIM ** -0.5 * LOG2E); o += 256
    k = w_in[..., o:o + 256]; o += 256
    v = w_in[..., o:o + 256]; o += 256
    f = w_in[..., o:o + N_HEADS]; o += N_HEADS
    rest = w_in[..., o:]
    f = jnp.pad(f, ((0, 0), (0, 0), (0, LANES - N_HEADS)))
    return jnp.concatenate([conf, q, k, v, rest, f], axis=-1).astype(BF16)


def _block_diag(w_pool):
    L, G, C, _ = w_pool.shape
    eye = jnp.eye(G, dtype=w_pool.dtype)
    return jnp.einsum("lgcd,gh->lgchd", w_pool, eye).reshape(L, G * C, G * C)


def _head_indicator():
    e = np.zeros((2 * D_GRP, LANES), np.float32)
    for d in range(2 * D_GRP):
        e[d, d // HEAD_DIM] = 1.0
    return jnp.asarray(e, BF16)


def kernel(x, p, g_mix_pre, w_in, b_forget, w_conf_dw, conf_ln_g, conf_ln_b, w_conf_pw, w_sc, w_pool, pool_scale, w_out, g_mix_post, g_mlp_pre, w_up, w_down, g_mlp_post, g_ple_pre, w_ple_gate, w_ple_proj, g_ple_post):
    L = w_in.shape[0]
    S = x.shape[1]
    win = _permute_w_in(w_in)
    bf = jnp.pad(b_forget, ((0, 0), (0, LANES - N_HEADS)))[:, None, :]
    wpw = w_conf_pw.astype(BF16)
    wpool = _block_diag(w_pool).astype(BF16)
    wo3 = jnp.concatenate([w_out[:, 0:256], w_out[:, 512:1024]], axis=1).astype(BF16)
    woa = w_out[:, 256:512].astype(BF16)
    wup = w_up.astype(BF16)
    wdn = w_down.astype(BF16)
    wgate = w_ple_gate.astype(BF16)
    wproj = w_ple_proj.astype(BF16)
    zero = jnp.zeros_like(g_mix_post)
    gains = jnp.stack([g_mix_post, g_mlp_pre, g_mlp_post, g_ple_pre, g_ple_post,
                       zero, zero, zero], axis=1)
    e2 = _head_indicator()

    h = x
    for i in range(L):
        q, k, vt, y3, ccol, crow, bstat = _front(
            h, g_mix_pre[i][None, :], win[i], bf[i], w_conf_dw[i],
            conf_ln_g[i][None, :], conf_ln_b[i][None, :], wpw[i], w_sc[i],
            wpool[i], pool_scale[i][None, :], e2)
        jlo = _first_needed_block(bstat, min(TQ, S))
        ya = _attention(jlo, q, k, vt, ccol, crow)
        h = _back(h, y3, ya, p, i, gains[i], wo3[i], woa[i], wup[i], wdn[i],
                  wgate[i], wproj[i])
    return h
```

```python
import functools

import jax
import jax.numpy as jnp
import numpy as np
from jax import lax
from jax.experimental import pallas as pl
from jax.experimental.pallas import tpu as pltpu

F32 = jnp.float32
BF16 = jnp.bfloat16

D_MODEL = 1024
D_GRP = 256
HEAD_DIM = 64
N_HEADS = D_GRP // HEAD_DIM
CONF_KERNEL = 31
SC_KERNEL = 3
POOL_WINDOWS = (2, 4, 8, 16)
D_FF = 4 * D_MODEL
D_PLE = 256
EPS = 1e-6
LANES = 128
SUBLANES = 8
LOG2E = 1.4426950408889634

C_CONF = 0
C_Q = 512
C_K = 768
C_V = 1024
C_SCH = 1280
C_SCB = 1536
C_SCC = 1792
C_POOL = 2048
C_F = 2304
D_Z = C_F + LANES

HALO_U = 32
HALO_SC = 8
HALO_P = 16
ROW_CHUNK = 64
STAT_ROWS = 128

TM_FRONT = 512
TM_BACK = 512
TQ = 256
VMEM_LIMIT = 56 * 1024 * 1024

SKIP_LOG2 = 104.0 * LOG2E + 1.0
NORM_SLACK = 1.02


def _rms(x, g):
    return x * lax.rsqrt(jnp.mean(x * x, axis=-1, keepdims=True) + EPS) * g


def _sigmoid(x):
    return 1.0 / (1.0 + jnp.exp(-x))


def _dot(a, b):
    return jnp.dot(a, b, preferred_element_type=F32)


def _front_kernel(h_ref, g_ref, win_ref, bf_ref, wdw_ref, lng_ref, lnb_ref,
                  wpw_ref, wsc_ref, wpool_ref, pscale_ref, e2_ref,
                  q_ref, k_ref, vt_ref, y3_ref, ccol_ref, crow_ref, bstat_ref,
                  u_ext, ch_ext, pv_ext, conv_buf, sc_buf, pool_buf, c_carry, *, tm):
    s = pl.program_id(1)

    @pl.when(s == 0)
    def _():
        u_ext[0:HALO_U, :] = jnp.zeros((HALO_U, D_GRP), F32)
        ch_ext[0:HALO_SC, :] = jnp.zeros((HALO_SC, D_GRP), F32)
        pv_ext[0:HALO_P, :] = jnp.zeros((HALO_P, D_GRP), F32)
        c_carry[...] = jnp.zeros((1, LANES), F32)

    h = h_ref[0]
    xn = _rms(h, g_ref[...]).astype(BF16)
    z = _dot(xn, win_ref[...])

    qb = z[:, C_Q:C_Q + D_GRP].astype(BF16)
    kb = z[:, C_K:C_K + D_GRP].astype(BF16)
    q_ref[0] = qb
    k_ref[0] = kb
    vt_ref[0] = z[:, C_V:C_V + D_GRP].T.astype(BF16)

    qf = qb.astype(F32)
    kf = kb.astype(F32)
    sq = jnp.concatenate([qf * qf, kf * kf], axis=1).astype(BF16)
    n2 = _dot(sq, e2_ref[...])

    f = z[:, C_F:C_F + LANES] + bf_ref[...]
    logf = (jnp.minimum(f, 0.0) - jnp.log(1.0 + jnp.exp(-jnp.abs(f)))) * LOG2E
    row = lax.broadcasted_iota(jnp.int32, (tm, LANES), 0)
    c = logf
    shift = 1
    while shift < tm:
        c = c + jnp.where(row >= shift, pltpu.roll(c, shift, axis=0), 0.0)
        shift *= 2
    c = c + c_carry[...]
    c_carry[...] = c[tm - 1:tm, :]
    ccol_ref[0] = c
    crow_ref[0] = c.T[0:SUBLANES, :]

    nb = tm // STAT_ROWS
    c3 = c.reshape(nb, STAT_ROWS, LANES)
    bstat_ref[0, 0, 0:nb, :] = jnp.max(n2.reshape(nb, STAT_ROWS, LANES), axis=1)
    bstat_ref[0, 0, nb:2 * nb, :] = c3[:, 0, :]
    bstat_ref[0, 0, 2 * nb:3 * nb, :] = c3[:, STAT_ROWS - 1, :]

    u = z[:, C_CONF:C_CONF + D_GRP] * _sigmoid(z[:, C_CONF + D_GRP:C_CONF + 2 * D_GRP])
    u_ext[HALO_U:HALO_U + tm, :] = u
    n_a = (CONF_KERNEL - 1) // SUBLANES + 1
    rc = ROW_CHUNK
    for r0 in range(0, tm, rc):
        base = HALO_U + r0 - SUBLANES
        ua = [u_ext[base - SUBLANES * a:base - SUBLANES * a + rc + SUBLANES, :]
              for a in range(n_a)]
        total = None
        for r in range(SUBLANES):
            part = None
            for a in range(n_a):
                d = SUBLANES * a + r
                if d > CONF_KERNEL - 1:
                    continue
                kk = CONF_KERNEL - 1 - d
                term = wdw_ref[kk:kk + 1, :] * ua[a]
                part = term if part is None else part + term
            if r:
                part = pltpu.roll(part, r, axis=0)
            part = part[SUBLANES:, :]
            total = part if total is None else total + part
        conv_buf[r0:r0 + rc, :] = total
    u_ext[0:HALO_U, :] = u_ext[tm:tm + HALO_U, :]
    cv = conv_buf[...]
    mu = jnp.mean(cv, axis=-1, keepdims=True)
    xc = cv - mu
    ln = xc * lax.rsqrt(jnp.mean(xc * xc, axis=-1, keepdims=True) + EPS)
    ln = ln * lng_ref[...] + lnb_ref[...]
    act = ln * _sigmoid(ln)
    y3_ref[0, :, 0:D_GRP] = _dot(act.astype(BF16), wpw_ref[...]).astype(BF16)

    ch_ext[HALO_SC:HALO_SC + tm, :] = z[:, C_SCC:C_SCC + D_GRP] * z[:, C_SCH:C_SCH + D_GRP]
    for r0 in range(0, tm, rc):
        xs = ch_ext[r0:r0 + rc + HALO_SC, :]
        acc = wsc_ref[SC_KERNEL - 1:SC_KERNEL, :] * xs
        for d in range(1, SC_KERNEL):
            kk = SC_KERNEL - 1 - d
            acc = acc + wsc_ref[kk:kk + 1, :] * pltpu.roll(xs, d, axis=0)
        sc_buf[r0:r0 + rc, :] = acc[HALO_SC:, :]
    ch_ext[0:HALO_SC, :] = ch_ext[tm:tm + HALO_SC, :]
    y3_ref[0, :, D_GRP:2 * D_GRP] = (z[:, C_SCB:C_SCB + D_GRP] * sc_buf[...]).astype(BF16)

    pv = z[:, C_POOL:C_POOL + D_GRP]
    pv_ext[HALO_P:HALO_P + tm, :] = pv
    lane = lax.broadcasted_iota(jnp.int32, (1, D_GRP), 1)
    grp = lane // (D_GRP // len(POOL_WINDOWS))
    for r0 in range(0, tm, rc):
        run = pv_ext[r0:r0 + rc + HALO_P, :]
        sel = None
        w = 1
        for gi, wg in enumerate(POOL_WINDOWS):
            while w < wg:
                run = run + pltpu.roll(run, w, axis=0)
                w *= 2
            sel = run if sel is None else jnp.where(grp >= gi, run, sel)
        pool_buf[r0:r0 + rc, :] = sel[HALO_P:, :]
    pv_ext[0:HALO_P, :] = pv_ext[tm:tm + HALO_P, :]
    wl = jnp.zeros((1, D_GRP), jnp.int32)
    for gi, wg in enumerate(POOL_WINDOWS):
        wl = jnp.where(grp == gi, wg, wl)
    pos = s * tm + lax.broadcasted_iota(jnp.int32, (tm, D_GRP), 0)
    cnt = jnp.minimum(pos + 1, wl).astype(F32)
    d = pool_buf[...] / cnt - pv
    yp = _dot(d.astype(BF16), wpool_ref[...]) * pscale_ref[...]
    y3_ref[0, :, 2 * D_GRP:3 * D_GRP] = yp.astype(BF16)


def _const_spec(shape):
    nd = len(shape)
    return pl.BlockSpec(shape, lambda *_: (0,) * nd)


def _front(h, g, win, bf, wdw, lng, lnb, wpw, wsc, wpool, pscale, e2):
    B, S, _ = h.shape
    tm = min(TM_FRONT, S)
    nb = tm // STAT_ROWS
    grid = (B, S // tm)
    tok = lambda w: pl.BlockSpec((1, tm, w), lambda b, s: (b, s, 0))
    out_shape = (
        jax.ShapeDtypeStruct((B, S, D_GRP), BF16),
        jax.ShapeDtypeStruct((B, S, D_GRP), BF16),
        jax.ShapeDtypeStruct((B, D_GRP, S), BF16),
        jax.ShapeDtypeStruct((B, S, 3 * D_GRP), BF16),
        jax.ShapeDtypeStruct((B, S, LANES), F32),
        jax.ShapeDtypeStruct((B, SUBLANES, S), F32),
        jax.ShapeDtypeStruct((B, S // tm, 3 * nb, LANES), F32),
    )
    out_specs = (
        tok(D_GRP), tok(D_GRP),
        pl.BlockSpec((1, D_GRP, tm), lambda b, s: (b, 0, s)),
        tok(3 * D_GRP), tok(LANES),
        pl.BlockSpec((1, SUBLANES, tm), lambda b, s: (b, 0, s)),
        pl.BlockSpec((1, 1, 3 * nb, LANES), lambda b, s: (b, s, 0, 0)),
    )
    consts = (g, win, bf, wdw, lng, lnb, wpw, wsc, wpool, pscale, e2)
    in_specs = [tok(D_MODEL)] + [_const_spec(a.shape) for a in consts]
    return pl.pallas_call(
        functools.partial(_front_kernel, tm=tm),
        grid=grid, in_specs=in_specs, out_specs=out_specs, out_shape=out_shape,
        scratch_shapes=[
            pltpu.VMEM((HALO_U + tm, D_GRP), F32),
            pltpu.VMEM((HALO_SC + tm, D_GRP), F32),
            pltpu.VMEM((HALO_P + tm, D_GRP), F32),
            pltpu.VMEM((tm, D_GRP), F32),
            pltpu.VMEM((tm, D_GRP), F32),
            pltpu.VMEM((tm, D_GRP), F32),
            pltpu.VMEM((1, LANES), F32),
        ],
        compiler_params=pltpu.CompilerParams(
            dimension_semantics=("arbitrary", "arbitrary"),
            vmem_limit_bytes=VMEM_LIMIT),
        name="front",
    )(h, *consts)


def _first_needed_block(bstat, tq):
    B, nt, nb3, _ = bstat.shape
    nb = nb3 // 3
    flat = lambda a: a.reshape(B, nt * nb, LANES)
    n2 = flat(bstat[:, :, 0:nb])
    cs = flat(bstat[:, :, nb:2 * nb])[..., 0:N_HEADS]
    ce = flat(bstat[:, :, 2 * nb:3 * nb])[..., 0:N_HEADS]
    r = tq // STAT_ROWS
    nq = nt * nb // r
    qn = jnp.sqrt(jnp.max(n2[..., 0:N_HEADS].reshape(B, nq, r, N_HEADS), axis=2))
    kn = jnp.sqrt(jnp.max(n2[..., N_HEADS:2 * N_HEADS], axis=1, keepdims=True))
    bound = 2.0 * NORM_SLACK * qn * kn + 1.0
    c_start = cs[:, ::r]
    c_end = ce[:, r - 1::r]
    gap = c_start[:, :, None, :] - c_end[:, None, :, :]
    ii = jnp.arange(nq)[:, None]
    jj = jnp.arange(nq)[None, :]
    skip = (gap + bound[:, :, None, :] <= -SKIP_LOG2) & (jj < ii)[None, :, :, None]
    first = jnp.min(jnp.where(skip, nq, jj[None, :, :, None]), axis=2)
    return jnp.min(first, axis=2).astype(jnp.int32)


def _attn_kernel(jlo_ref, q_ref, k_ref, vt_ref, ccol_ref, crow_ref, o_ref, *, tq):
    b = pl.program_id(0)
    qi = pl.program_id(1)
    qs = pl.multiple_of(qi * tq, tq)
    key_i = lax.broadcasted_iota(jnp.int32, (tq, N_HEADS * tq), 0)
    qry_i = lax.broadcasted_iota(jnp.int32, (tq, N_HEADS * tq), 1) % tq
    causal_t = key_i <= qry_i
    lane = lax.broadcasted_iota(jnp.int32, (1, LANES), 1)
    nt = (((1,), (1,)), ((), ()))
    hpp = LANES // HEAD_DIM
    n_tiles = D_GRP // LANES
    hcols = lambda a, head: a[:, head * tq:(head + 1) * tq]
    hrows = lambda a, head: a[head * HEAD_DIM:(head + 1) * HEAD_DIM, :]

    q_tiles = []
    for t in range(n_tiles):
        q_pair = q_ref[0, :, t * LANES:(t + 1) * LANES]
        q_tiles.append(jnp.concatenate(
            [jnp.where((lane // HEAD_DIM) == hh, q_pair, jnp.zeros_like(q_pair))
             for hh in range(hpp)], axis=0))
    c_refs = [crow_ref[0, head:head + 1, pl.ds(qs, LANES)][:, 0:1]
              for head in range(N_HEADS)]

    def block(j, carry, masked):
        m, l, acc = carry
        ks = pl.multiple_of(j * tq, tq)
        parts = []
        for t in range(n_tiles):
            kb = k_ref[0, pl.ds(ks, tq), t * LANES:(t + 1) * LANES]
            s2 = lax.dot_general(kb, q_tiles[t], nt, preferred_element_type=F32)
            for hh in range(hpp):
                head = t * hpp + hh
                ck = ccol_ref[0, pl.ds(ks, tq), head:head + 1] - c_refs[head]
                parts.append(hcols(s2, hh) - ck)
        st = jnp.concatenate(parts, axis=1)
        if masked:
            st = jnp.where(causal_t, st, -jnp.inf)
        m_new = jnp.maximum(m, jnp.max(st, axis=0, keepdims=True))
        alpha = jnp.exp2(m - m_new)
        p = jnp.exp2(st - m_new)
        l = alpha * l + jnp.sum(p, axis=0, keepdims=True)
        pb = p.astype(BF16)
        accs = []
        for head in range(N_HEADS):
            vtb = vt_ref[0, head * HEAD_DIM:(head + 1) * HEAD_DIM, pl.ds(ks, tq)]
            accs.append(hcols(alpha, head) * hrows(acc, head) + _dot(vtb, hcols(pb, head)))
        return m_new, l, jnp.concatenate(accs, axis=0)

    init = (jnp.full((1, N_HEADS * tq), -jnp.inf, F32),
            jnp.zeros((1, N_HEADS * tq), F32), jnp.zeros((D_GRP, tq), F32))
    carry = lax.fori_loop(jlo_ref[b, qi], qi, lambda j, cr: block(j, cr, False), init)
    _, l, acc = block(qi, carry, True)
    o_t = jnp.concatenate([hrows(acc, head) / hcols(l, head)
                           for head in range(N_HEADS)], axis=0)
    o_ref[0] = o_t.T.astype(BF16)


def _attention(jlo, q, k, vt, ccol, crow):
    B, S, _ = q.shape
    tq = min(TQ, S)
    grid_spec = pltpu.PrefetchScalarGridSpec(
        num_scalar_prefetch=1,
        grid=(B, S // tq),
        in_specs=[
            pl.BlockSpec((1, tq, D_GRP), lambda b, i, jlo: (b, i, 0)),
            pl.BlockSpec((1, S, D_GRP), lambda b, i, jlo: (b, 0, 0)),
            pl.BlockSpec((1, D_GRP, S), lambda b, i, jlo: (b, 0, 0)),
            pl.BlockSpec((1, S, LANES), lambda b, i, jlo: (b, 0, 0)),
            pl.BlockSpec((1, SUBLANES, S), lambda b, i, jlo: (b, 0, 0)),
        ],
        out_specs=pl.BlockSpec((1, tq, D_GRP), lambda b, i, jlo: (b, i, 0)),
    )
    return pl.pallas_call(
        functools.partial(_attn_kernel, tq=tq),
        grid_spec=grid_spec,
        out_shape=jax.ShapeDtypeStruct((B, S, D_GRP), BF16),
        compiler_params=pltpu.CompilerParams(
            dimension_semantics=("arbitrary", "arbitrary"),
            vmem_limit_bytes=VMEM_LIMIT),
        name="attn",
    )(jlo, q, k, vt, ccol, crow)


def _back_kernel(h_ref, y3_ref, ya_ref, p_ref, gains_ref, wo3_ref, woa_ref,
                 wup_ref, wdn_ref, wgate_ref, wproj_ref, o_ref):
    g_mix_post = gains_ref[0:1, :]
    g_mlp_pre = gains_ref[1:2, :]
    g_mlp_post = gains_ref[2:3, :]
    g_ple_pre = gains_ref[3:4, :]
    g_ple_post = gains_ref[4:5, :]

    h = h_ref[0]
    mix = _dot(y3_ref[0], wo3_ref[...]) + _dot(ya_ref[0], woa_ref[...])
    h = h + _rms(mix, g_mix_post)

    hn = _rms(h, g_mlp_pre).astype(BF16)
    up = jnp.maximum(_dot(hn, wup_ref[...]), 0.0)
    ff = _dot((up * up).astype(BF16), wdn_ref[...])
    h = h + _rms(ff, g_mlp_post)

    gate = _sigmoid(_dot(_rms(h, g_ple_pre).astype(BF16), wgate_ref[...]))
    e = _dot(p_ref[0, 0].astype(BF16), wproj_ref[...]) * gate
    o_ref[0] = h + _rms(e, g_ple_post)


def _back(h, y3, ya, p, layer, gains, wo3, woa, wup, wdn, wgate, wproj):
    B, S, _ = h.shape
    tm = min(TM_BACK, S)
    grid = (B, S // tm)
    tok = lambda w: pl.BlockSpec((1, tm, w), lambda b, s: (b, s, 0))
    wspec = lambda a: pl.BlockSpec(a.shape, lambda b, s: (0,) * a.ndim,
                                   pipeline_mode=pl.Buffered(1))
    in_specs = [
        tok(D_MODEL), tok(3 * D_GRP), tok(D_GRP),
        pl.BlockSpec((1, 1, tm, D_PLE), lambda b, s: (layer, b, s, 0)),
    ] + [wspec(a) for a in (gains, wo3, woa, wup, wdn, wgate, wproj)]
    return pl.pallas_call(
        _back_kernel,
        grid=grid, in_specs=in_specs, out_specs=tok(D_MODEL),
        out_shape=jax.ShapeDtypeStruct((B, S, D_MODEL), F32),
        compiler_params=pltpu.CompilerParams(
            dimension_semantics=("arbitrary", "arbitrary"),
            vmem_limit_bytes=VMEM_LIMIT),
        name="back",
    )(h, y3, ya, p, gains, wo3, woa, wup, wdn, wgate, wproj)


def _permute_w_in(w_in):
    o = 0
    conf = w_in[..., o:o + 512]; o += 512
    q = w_in[..., o:o + 256] * (HEAD_DIM ** -0.5 * LOG2E); o += 256
    k = w_in[..., o:o + 256]; o += 256
    v = w_in[..., o:o + 256]; o += 256
    f = w_in[..., o:o + N_HEADS]; o += N_HEADS
    rest = w_in[..., o:]
    f = jnp.pad(f, ((0, 0), (0, 0), (0, LANES - N_HEADS)))
    return jnp.concatenate([conf, q, k, v, rest, f], axis=-1).astype(BF16)


def _block_diag(w_pool):
    L, G, C, _ = w_pool.shape
    eye = jnp.eye(G, dtype=w_pool.dtype)
    return jnp.einsum("lgcd,gh->lgchd", w_pool, eye).reshape(L, G * C, G * C)


def _head_indicator():
    e = np.zeros((2 * D_GRP, LANES), np.float32)
    for d in range(2 * D_GRP):
        e[d, d // HEAD_DIM] = 1.0
    return jnp.asarray(e, BF16)


def kernel(x, p, g_mix_pre, w_in, b_forget, w_conf_dw, conf_ln_g, conf_ln_b, w_conf_pw, w_sc, w_pool, pool_scale, w_out, g_mix_post, g_mlp_pre, w_up, w_down, g_mlp_post, g_ple_pre, w_ple_gate, w_ple_proj, g_ple_post):
    L = w_in.shape[0]
    S = x.shape[1]
    win = _permute_w_in(w_in)
    bf = jnp.pad(b_forget, ((0, 0), (0, LANES - N_HEADS)))[:, None, :]
    wpw = w_conf_pw.astype(BF16)
    wpool = _block_diag(w_pool).astype(BF16)
    wo3 = jnp.concatenate([w_out[:, 0:256], w_out[:, 512:1024]], axis=1).astype(BF16)
    woa = w_out[:, 256:512].astype(BF16)
    wup = w_up.astype(BF16)
    wdn = w_down.astype(BF16)
    wgate = w_ple_gate.astype(BF16)
    wproj = w_ple_proj.astype(BF16)
    zero = jnp.zeros_like(g_mix_post)
    gains = jnp.stack([g_mix_post, g_mlp_pre, g_mlp_post, g_ple_pre, g_ple_post,
                       zero, zero, zero], axis=1)
    e2 = _head_indicator()

    h = x
    for i in range(L):
        q, k, vt, y3, ccol, crow, bstat = _front(
            h, g_mix_pre[i][None, :], win[i], bf[i], w_conf_dw[i],
            conf_ln_g[i][None, :], conf_ln_b[i][None, :], wpw[i], w_sc[i],
            wpool[i], pool_scale[i][None, :], e2)
        jlo = _first_needed_block(bstat, min(TQ, S))
        ya = _attention(jlo, q, k, vt, ccol, crow)
        h = _back(h, y3, ya, p, i, gains[i], wo3[i], woa[i], wup[i], wdn[i],
                  wgate[i], wproj[i])
    return h
```

```python
import functools

import jax
import jax.numpy as jnp
import numpy as np
from jax import lax
from jax.experimental import pallas as pl
from jax.experimental.pallas import tpu as pltpu

F32 = jnp.float32
BF16 = jnp.bfloat16

D_MODEL = 1024
D_GRP = 256
HEAD_DIM = 64
N_HEADS = D_GRP // HEAD_DIM
CONF_KERNEL = 31
SC_KERNEL = 3
POOL_WINDOWS = (2, 4, 8, 16)
D_FF = 4 * D_MODEL
D_PLE = 256
EPS = 1e-6
LANES = 128
SUBLANES = 8
LOG2E = 1.4426950408889634

C_CONF = 0
C_Q = 512
C_K = 768
C_V = 1024
C_SCH = 1280
C_SCB = 1536
C_SCC = 1792
C_POOL = 2048
C_F = 2304
D_Z = C_F + LANES

HALO_U = 32
HALO_SC = 8
HALO_P = 16
ROW_CHUNK = 64
CONV_ROWS = 64
STAT_ROWS = 128

TM_FRONT = 1024
TM_BACK = 512
TQ = 256
VMEM_LIMIT = 56 * 1024 * 1024

SKIP_LOG2 = 104.0 * LOG2E + 1.0
NORM_SLACK = 1.02


def _rms(x, g):
    return x * lax.rsqrt(jnp.mean(x * x, axis=-1, keepdims=True) + EPS) * g


def _sigmoid(x):
    return 1.0 / (1.0 + jnp.exp(-x))


def _dot(a, b):
    return jnp.dot(a, b, preferred_element_type=F32)


def _front_kernel(h_ref, g_ref, win_ref, bf_ref, wdw_ref, lng_ref, lnb_ref,
                  wpw_ref, wsc_ref, wpool_ref, pscale_ref, e2_ref,
                  q_ref, k_ref, vt_ref, y3_ref, ccol_ref, crow_ref, bstat_ref,
                  u_ext, ch_ext, pv_ext, conv_buf, sc_buf, pool_buf, c_carry, *, tm):
    s = pl.program_id(1)

    @pl.when(s == 0)
    def _():
        u_ext[0:HALO_U, :] = jnp.zeros((HALO_U, D_GRP), F32)
        ch_ext[0:HALO_SC, :] = jnp.zeros((HALO_SC, D_GRP), F32)
        pv_ext[0:HALO_P, :] = jnp.zeros((HALO_P, D_GRP), F32)
        c_carry[...] = jnp.zeros((SUBLANES, LANES), F32)

    h = h_ref[0]
    xn = _rms(h, g_ref[...]).astype(BF16)
    z = _dot(xn, win_ref[...])

    qb = z[:, C_Q:C_Q + D_GRP].astype(BF16)
    kb = z[:, C_K:C_K + D_GRP].astype(BF16)
    q_ref[0] = qb
    k_ref[0] = kb
    vt_ref[0] = z[:, C_V:C_V + D_GRP].T.astype(BF16)

    qf = qb.astype(F32)
    kf = kb.astype(F32)
    sq = jnp.concatenate([qf * qf, kf * kf], axis=1).astype(BF16)
    n2 = _dot(sq, e2_ref[...])

    f = z[:, C_F:C_F + LANES].T[0:SUBLANES, :] + bf_ref[:, 0:1]
    logf = (jnp.minimum(f, 0.0) - jnp.log(1.0 + jnp.exp(-jnp.abs(f)))) * LOG2E
    pos_l = lax.broadcasted_iota(jnp.int32, (SUBLANES, tm), 1)
    ct = logf
    shift = 1
    while shift < tm:
        ct = ct + jnp.where(pos_l >= shift, pltpu.roll(ct, shift, axis=1), 0.0)
        shift *= 2
    ct = ct + c_carry[:, 0:1]
    c_carry[...] = jnp.broadcast_to(ct[:, tm - 1:tm], (SUBLANES, LANES))
    crow_ref[0] = ct
    c = jnp.concatenate([ct, jnp.zeros((LANES - SUBLANES, tm), F32)], axis=0).T
    ccol_ref[0] = c

    nb = tm // STAT_ROWS
    c3 = c.reshape(nb, STAT_ROWS, LANES)
    bstat_ref[0, 0, 0:nb, :] = jnp.max(n2.reshape(nb, STAT_ROWS, LANES), axis=1)
    bstat_ref[0, 0, nb:2 * nb, :] = c3[:, 0, :]
    bstat_ref[0, 0, 2 * nb:3 * nb, :] = c3[:, STAT_ROWS - 1, :]

    u = z[:, C_CONF:C_CONF + D_GRP] * _sigmoid(z[:, C_CONF + D_GRP:C_CONF + 2 * D_GRP])
    u_ext[HALO_U:HALO_U + tm, :] = u
    n_a = (CONF_KERNEL - 1) // SUBLANES + 1
    rc = ROW_CHUNK
    for r0, l0 in [(r0, l0) for r0 in range(0, tm, CONV_ROWS)
                   for l0 in range(0, D_GRP, LANES)]:
        base = HALO_U + r0 - SUBLANES
        ls = slice(l0, l0 + LANES)
        ua = [u_ext[base - SUBLANES * a:base - SUBLANES * a + CONV_ROWS + SUBLANES, ls]
              for a in range(n_a)]
        total = None
        for r in range(SUBLANES):
            part = None
            for a in range(n_a):
                d = SUBLANES * a + r
                if d > CONF_KERNEL - 1:
                    continue
                kk = CONF_KERNEL - 1 - d
                term = wdw_ref[kk:kk + 1, ls] * ua[a]
                part = term if part is None else part + term
            if r:
                part = pltpu.roll(part, r, axis=0)
            part = part[SUBLANES:, :]
            total = part if total is None else total + part
        conv_buf[r0:r0 + CONV_ROWS, ls] = total
    u_ext[0:HALO_U, :] = u_ext[tm:tm + HALO_U, :]
    cv = conv_buf[...]
    mu = jnp.mean(cv, axis=-1, keepdims=True)
    xc = cv - mu
    ln = xc * lax.rsqrt(jnp.mean(xc * xc, axis=-1, keepdims=True) + EPS)
    ln = ln * lng_ref[...] + lnb_ref[...]
    act = ln * _sigmoid(ln)
    y3_ref[0, :, 0:D_GRP] = _dot(act.astype(BF16), wpw_ref[...]).astype(BF16)

    ch_ext[HALO_SC:HALO_SC + tm, :] = z[:, C_SCC:C_SCC + D_GRP] * z[:, C_SCH:C_SCH + D_GRP]
    for r0 in range(0, tm, rc):
        xs = ch_ext[r0:r0 + rc + HALO_SC, :]
        acc = wsc_ref[SC_KERNEL - 1:SC_KERNEL, :] * xs
        for d in range(1, SC_KERNEL):
            kk = SC_KERNEL - 1 - d
            acc = acc + wsc_ref[kk:kk + 1, :] * pltpu.roll(xs, d, axis=0)
        sc_buf[r0:r0 + rc, :] = acc[HALO_SC:, :]
    ch_ext[0:HALO_SC, :] = ch_ext[tm:tm + HALO_SC, :]
    y3_ref[0, :, D_GRP:2 * D_GRP] = (z[:, C_SCB:C_SCB + D_GRP] * sc_buf[...]).astype(BF16)

    pv = z[:, C_POOL:C_POOL + D_GRP]
    pv_ext[HALO_P:HALO_P + tm, :] = pv
    lane = lax.broadcasted_iota(jnp.int32, (1, D_GRP), 1)
    grp = lane // (D_GRP // len(POOL_WINDOWS))
    for r0 in range(0, tm, rc):
        run = pv_ext[r0:r0 + rc + HALO_P, :]
        sel = None
        w = 1
        for gi, wg in enumerate(POOL_WINDOWS):
            while w < wg:
                run = run + pltpu.roll(run, w, axis=0)
                w *= 2
            sel = run if sel is None else jnp.where(grp >= gi, run, sel)
        pool_buf[r0:r0 + rc, :] = sel[HALO_P:, :]
    pv_ext[0:HALO_P, :] = pv_ext[tm:tm + HALO_P, :]
    wl = jnp.zeros((1, D_GRP), jnp.int32)
    for gi, wg in enumerate(POOL_WINDOWS):
        wl = jnp.where(grp == gi, wg, wl)
    pos = s * tm + lax.broadcasted_iota(jnp.int32, (tm, D_GRP), 0)
    cnt = jnp.minimum(pos + 1, wl).astype(F32)
    d = pool_buf[...] / cnt - pv
    yp = _dot(d.astype(BF16), wpool_ref[...]) * pscale_ref[...]
    y3_ref[0, :, 2 * D_GRP:3 * D_GRP] = yp.astype(BF16)


def _const_spec(shape):
    nd = len(shape)
    return pl.BlockSpec(shape, lambda *_: (0,) * nd)


def _front(h, g, win, bf, wdw, lng, lnb, wpw, wsc, wpool, pscale, e2):
    B, S, _ = h.shape
    tm = min(TM_FRONT, S)
    nb = tm // STAT_ROWS
    grid = (B, S // tm)
    tok = lambda w: pl.BlockSpec((1, tm, w), lambda b, s: (b, s, 0))
    out_shape = (
        jax.ShapeDtypeStruct((B, S, D_GRP), BF16),
        jax.ShapeDtypeStruct((B, S, D_GRP), BF16),
        jax.ShapeDtypeStruct((B, D_GRP, S), BF16),
        jax.ShapeDtypeStruct((B, S, 3 * D_GRP), BF16),
        jax.ShapeDtypeStruct((B, S, LANES), F32),
        jax.ShapeDtypeStruct((B, SUBLANES, S), F32),
        jax.ShapeDtypeStruct((B, S // tm, 3 * nb, LANES), F32),
    )
    out_specs = (
        tok(D_GRP), tok(D_GRP),
        pl.BlockSpec((1, D_GRP, tm), lambda b, s: (b, 0, s)),
        tok(3 * D_GRP), tok(LANES),
        pl.BlockSpec((1, SUBLANES, tm), lambda b, s: (b, 0, s)),
        pl.BlockSpec((1, 1, 3 * nb, LANES), lambda b, s: (b, s, 0, 0)),
    )
    consts = (g, win, bf, wdw, lng, lnb, wpw, wsc, wpool, pscale, e2)
    in_specs = [tok(D_MODEL)] + [_const_spec(a.shape) for a in consts]
    return pl.pallas_call(
        functools.partial(_front_kernel, tm=tm),
        grid=grid, in_specs=in_specs, out_specs=out_specs, out_shape=out_shape,
        scratch_shapes=[
            pltpu.VMEM((HALO_U + tm, D_GRP), F32),
            pltpu.VMEM((HALO_SC + tm, D_GRP), F32),
            pltpu.VMEM((HALO_P + tm, D_GRP), F32),
            pltpu.VMEM((tm, D_GRP), F32),
            pltpu.VMEM((tm, D_GRP), F32),
            pltpu.VMEM((tm, D_GRP), F32),
            pltpu.VMEM((SUBLANES, LANES), F32),
        ],
        compiler_params=pltpu.CompilerParams(
            dimension_semantics=("arbitrary", "arbitrary"),
            vmem_limit_bytes=VMEM_LIMIT),
        name="front",
    )(h, *consts)


def _first_needed_block(bstat, tq):
    B, nt, nb3, _ = bstat.shape
    nb = nb3 // 3
    flat = lambda a: a.reshape(B, nt * nb, LANES)
    n2 = flat(bstat[:, :, 0:nb])
    cs = flat(bstat[:, :, nb:2 * nb])[..., 0:N_HEADS]
    ce = flat(bstat[:, :, 2 * nb:3 * nb])[..., 0:N_HEADS]
    r = tq // STAT_ROWS
    nq = nt * nb // r
    qn = jnp.sqrt(jnp.max(n2[..., 0:N_HEADS].reshape(B, nq, r, N_HEADS), axis=2))
    kn = jnp.sqrt(jnp.max(n2[..., N_HEADS:2 * N_HEADS], axis=1, keepdims=True))
    bound = 2.0 * NORM_SLACK * qn * kn + 1.0
    c_start = cs[:, ::r]
    c_end = ce[:, r - 1::r]
    gap = c_start[:, :, None, :] - c_end[:, None, :, :]
    ii = jnp.arange(nq)[:, None]
    jj = jnp.arange(nq)[None, :]
    skip = (gap + bound[:, :, None, :] <= -SKIP_LOG2) & (jj < ii)[None, :, :, None]
    first = jnp.min(jnp.where(skip, nq, jj[None, :, :, None]), axis=2)
    return jnp.min(first, axis=2).astype(jnp.int32)


def _attn_kernel(jlo_ref, q_ref, k_ref, vt_ref, ccol_ref, crow_ref, o_ref, *, tq):
    b = pl.program_id(0)
    qi = pl.program_id(1)
    qs = pl.multiple_of(qi * tq, tq)
    lane = lax.broadcasted_iota(jnp.int32, (1, LANES), 1)
    nt = (((1,), (1,)), ((), ()))
    hpp = LANES // HEAD_DIM
    n_tiles = D_GRP // LANES
    hcols = lambda a, head: a[:, head * tq:(head + 1) * tq]
    hrows = lambda a, head: a[head * HEAD_DIM:(head + 1) * HEAD_DIM, :]

    q_tiles = []
    for t in range(n_tiles):
        q_pair = q_ref[0, :, t * LANES:(t + 1) * LANES]
        q_tiles.append(jnp.concatenate(
            [jnp.where((lane // HEAD_DIM) == hh, q_pair, jnp.zeros_like(q_pair))
             for hh in range(hpp)], axis=0))
    c_refs = [crow_ref[0, head:head + 1, pl.ds(qs, LANES)][:, 0:1]
              for head in range(N_HEADS)]

    def scores(starts, prev_live=None):
        parts = []
        for t in range(n_tiles):
            kb = jnp.concatenate([k_ref[0, pl.ds(ks, tq), t * LANES:(t + 1) * LANES]
                                  for ks in starts], axis=0)
            s2 = lax.dot_general(kb, q_tiles[t], nt, preferred_element_type=F32)
            for hh in range(hpp):
                head = t * hpp + hh
                cks = [ccol_ref[0, pl.ds(ks, tq), head:head + 1] - c_refs[head]
                       for ks in starts]
                if prev_live is not None:
                    cks[0] = jnp.where(prev_live, cks[0], jnp.inf)
                parts.append(hcols(s2, hh) - jnp.concatenate(cks, axis=0))
        return jnp.concatenate(parts, axis=1)

    def weighted_values(pb, starts):
        outs = []
        for head in range(N_HEADS):
            vtb = jnp.concatenate(
                [vt_ref[0, head * HEAD_DIM:(head + 1) * HEAD_DIM, pl.ds(ks, tq)]
                 for ks in starts], axis=1)
            outs.append(_dot(vtb, hcols(pb, head)))
        return outs

    prev_start = pl.multiple_of(jnp.maximum(qi - 1, 0) * tq, tq)
    st = scores([prev_start, qs], prev_live=qi > 0)
    key_i = lax.broadcasted_iota(jnp.int32, (2 * tq, N_HEADS * tq), 0)
    qry_i = lax.broadcasted_iota(jnp.int32, (2 * tq, N_HEADS * tq), 1) % tq
    st = jnp.where(key_i - tq <= qry_i, st, -jnp.inf)
    m = jnp.max(st, axis=0, keepdims=True)
    p = jnp.exp2(st - m)
    l = jnp.sum(p, axis=0, keepdims=True)
    acc = jnp.concatenate(weighted_values(p.astype(BF16), [prev_start, qs]), axis=0)

    def block(j, carry):
        m, l, acc = carry
        ks = pl.multiple_of(j * tq, tq)
        st = scores([ks])
        m_new = jnp.maximum(m, jnp.max(st, axis=0, keepdims=True))
        alpha = jnp.exp2(m - m_new)
        p = jnp.exp2(st - m_new)
        l = alpha * l + jnp.sum(p, axis=0, keepdims=True)
        pv = weighted_values(p.astype(BF16), [ks])
        accs = [hcols(alpha, head) * hrows(acc, head) + pv[head]
                for head in range(N_HEADS)]
        return m_new, l, jnp.concatenate(accs, axis=0)

    _, l, acc = lax.fori_loop(jlo_ref[b, qi], qi - 1, block, (m, l, acc))
    o_t = jnp.concatenate([hrows(acc, head) / hcols(l, head)
                           for head in range(N_HEADS)], axis=0)
    o_ref[0] = o_t.T.astype(BF16)


def _attention(jlo, q, k, vt, ccol, crow):
    B, S, _ = q.shape
    tq = min(TQ, S)
    grid_spec = pltpu.PrefetchScalarGridSpec(
        num_scalar_prefetch=1,
        grid=(B, S // tq),
        in_specs=[
            pl.BlockSpec((1, tq, D_GRP), lambda b, i, jlo: (b, i, 0)),
            pl.BlockSpec((1, S, D_GRP), lambda b, i, jlo: (b, 0, 0)),
            pl.BlockSpec((1, D_GRP, S), lambda b, i, jlo: (b, 0, 0)),
            pl.BlockSpec((1, S, LANES), lambda b, i, jlo: (b, 0, 0)),
            pl.BlockSpec((1, SUBLANES, S), lambda b, i, jlo: (b, 0, 0)),
        ],
        out_specs=pl.BlockSpec((1, tq, D_GRP), lambda b, i, jlo: (b, i, 0)),
    )
    return pl.pallas_call(
        functools.partial(_attn_kernel, tq=tq),
        grid_spec=grid_spec,
        out_shape=jax.ShapeDtypeStruct((B, S, D_GRP), BF16),
        compiler_params=pltpu.CompilerParams(
            dimension_semantics=("arbitrary", "arbitrary"),
            vmem_limit_bytes=VMEM_LIMIT),
        name="attn",
    )(jlo, q, k, vt, ccol, crow)


def _back_kernel(h_ref, y3_ref, ya_ref, p_ref, gains_ref, wo3_ref, woa_ref,
                 wup_ref, wdn_ref, wgate_ref, wproj_ref, o_ref):
    g_mix_post = gains_ref[0:1, :]
    g_mlp_pre = gains_ref[1:2, :]
    g_mlp_post = gains_ref[2:3, :]
    g_ple_pre = gains_ref[3:4, :]
    g_ple_post = gains_ref[4:5, :]

    h = h_ref[0]
    mix = _dot(y3_ref[0], wo3_ref[...]) + _dot(ya_ref[0], woa_ref[...])
    h = h + _rms(mix, g_mix_post)

    hn = _rms(h, g_mlp_pre).astype(BF16)
    up = jnp.maximum(_dot(hn, wup_ref[...]), 0.0)
    ff = _dot((up * up).astype(BF16), wdn_ref[...])
    h = h + _rms(ff, g_mlp_post)

    gate = _sigmoid(_dot(_rms(h, g_ple_pre).astype(BF16), wgate_ref[...]))
    e = _dot(p_ref[0, 0].astype(BF16), wproj_ref[...]) * gate
    o_ref[0] = h + _rms(e, g_ple_post)


def _back(h, y3, ya, p, layer, gains, wo3, woa, wup, wdn, wgate, wproj):
    B, S, _ = h.shape
    tm = min(TM_BACK, S)
    grid = (B, S // tm)
    tok = lambda w: pl.BlockSpec((1, tm, w), lambda b, s: (b, s, 0))
    wspec = lambda a: pl.BlockSpec(a.shape, lambda b, s: (0,) * a.ndim,
                                   pipeline_mode=pl.Buffered(1))
    in_specs = [
        tok(D_MODEL), tok(3 * D_GRP), tok(D_GRP),
        pl.BlockSpec((1, 1, tm, D_PLE), lambda b, s: (layer, b, s, 0)),
    ] + [wspec(a) for a in (gains, wo3, woa, wup, wdn, wgate, wproj)]
    return pl.pallas_call(
        _back_kernel,
        grid=grid, in_specs=in_specs, out_specs=tok(D_MODEL),
        out_shape=jax.ShapeDtypeStruct((B, S, D_MODEL), F32),
        compiler_params=pltpu.CompilerParams(
            dimension_semantics=("arbitrary", "arbitrary"),
            vmem_limit_bytes=VMEM_LIMIT),
        name="back",
    )(h, y3, ya, p, gains, wo3, woa, wup, wdn, wgate, wproj)


def _permute_w_in(w_in):
    o = 0
    conf = w_in[..., o:o + 512]; o += 512
    q = w_in[..., o:o + 256] * (HEAD_DIM ** -0.5 * LOG2E); o += 256
    k = w_in[..., o:o + 256]; o += 256
    v = w_in[..., o:o + 256]; o += 256
    f = w_in[..., o:o + N_HEADS]; o += N_HEADS
    rest = w_in[..., o:]
    f = jnp.pad(f, ((0, 0), (0, 0), (0, LANES - N_HEADS)))
    return jnp.concatenate([conf, q, k, v, rest, f], axis=-1).astype(BF16)


def _block_diag(w_pool):
    L, G, C, _ = w_pool.shape
    eye = jnp.eye(G, dtype=w_pool.dtype)
    return jnp.einsum("lgcd,gh->lgchd", w_pool, eye).reshape(L, G * C, G * C)


def _head_indicator():
    e = np.zeros((2 * D_GRP, LANES), np.float32)
    for d in range(2 * D_GRP):
        e[d, d // HEAD_DIM] = 1.0
    return jnp.asarray(e, BF16)


def kernel(x, p, g_mix_pre, w_in, b_forget, w_conf_dw, conf_ln_g, conf_ln_b, w_conf_pw, w_sc, w_pool, pool_scale, w_out, g_mix_post, g_mlp_pre, w_up, w_down, g_mlp_post, g_ple_pre, w_ple_gate, w_ple_proj, g_ple_post):
    L = w_in.shape[0]
    S = x.shape[1]
    win = _permute_w_in(w_in)
    bf = jnp.broadcast_to(jnp.pad(b_forget, ((0, 0), (0, SUBLANES - N_HEADS)))[:, :, None],
                          (L, SUBLANES, LANES))
    wpw = w_conf_pw.astype(BF16)
    wpool = _block_diag(w_pool).astype(BF16)
    wo3 = jnp.concatenate([w_out[:, 0:256], w_out[:, 512:1024]], axis=1).astype(BF16)
    woa = w_out[:, 256:512].astype(BF16)
    wup = w_up.astype(BF16)
    wdn = w_down.astype(BF16)
    wgate = w_ple_gate.astype(BF16)
    wproj = w_ple_proj.astype(BF16)
    zero = jnp.zeros_like(g_mix_post)
    gains = jnp.stack([g_mix_post, g_mlp_pre, g_mlp_post, g_ple_pre, g_ple_post,
                       zero, zero, zero], axis=1)
    e2 = _head_indicator()

    h = x
    for i in range(L):
        q, k, vt, y3, ccol, crow, bstat = _front(
            h, g_mix_pre[i][None, :], win[i], bf[i], w_conf_dw[i],
            conf_ln_g[i][None, :], conf_ln_b[i][None, :], wpw[i], w_sc[i],
            wpool[i], pool_scale[i][None, :], e2)
        jlo = _first_needed_block(bstat, min(TQ, S))
        ya = _attention(jlo, q, k, vt, ccol, crow)
        h = _back(h, y3, ya, p, i, gains[i], wo3[i], woa[i], wup[i], wdn[i],
                  wgate[i], wproj[i])
    return h
```

```python
import functools

import jax
import jax.numpy as jnp
import numpy as np
from jax import lax
from jax.experimental import pallas as pl
from jax.experimental.pallas import tpu as pltpu

F32 = jnp.float32
BF16 = jnp.bfloat16

D_MODEL = 1024
D_GRP = 256
HEAD_DIM = 64
N_HEADS = D_GRP // HEAD_DIM
CONF_KERNEL = 31
SC_KERNEL = 3
POOL_WINDOWS = (2, 4, 8, 16)
D_FF = 4 * D_MODEL
D_PLE = 256
EPS = 1e-6
LANES = 128
SUBLANES = 8
LOG2E = 1.4426950408889634

C_CONF = 0
C_Q = 512
C_K = 768
C_V = 1024
C_SCH = 1280
C_SCB = 1536
C_SCC = 1792
C_POOL = 2048
C_F = 2304
D_Z = C_F + LANES

HALO_U = 32
HALO_SC = 8
HALO_P = 16
ROW_CHUNK = 64
CONV_ROWS = 64
STAT_ROWS = 128

TM_FRONT = 1024
TM_BACK = 512
TQ = 256
VMEM_LIMIT = 56 * 1024 * 1024

SKIP_LOG2 = 104.0 * LOG2E + 1.0
NORM_SLACK = 1.02


def _inv_rms(x):
    return lax.rsqrt(jnp.mean(x * x, axis=-1, keepdims=True) + EPS)


def _rms(x, g):
    return x * _inv_rms(x) * g


def _sigmoid(x):
    return 1.0 / (1.0 + jnp.exp(-x))


def _dot(a, b):
    return jnp.dot(a, b, preferred_element_type=F32)


def _front_kernel(h_ref, g_ref, win_ref, bf_ref, wdw_ref, lng_ref, lnb_ref,
                  wpw_ref, wsc_ref, wpool_ref, pscale_ref, e2_ref,
                  q_ref, k_ref, vt_ref, y3_ref, ccol_ref, crow_ref, bstat_ref,
                  u_ext, ch_ext, pv_ext, conv_buf, sc_buf, pool_buf, c_carry, *, tm):
    s = pl.program_id(1)

    @pl.when(s == 0)
    def _():
        u_ext[0:HALO_U, :] = jnp.zeros((HALO_U, D_GRP), F32)
        ch_ext[0:HALO_SC, :] = jnp.zeros((HALO_SC, D_GRP), F32)
        pv_ext[0:HALO_P, :] = jnp.zeros((HALO_P, D_GRP), F32)
        c_carry[...] = jnp.zeros((SUBLANES, LANES), F32)

    h = h_ref[0]
    xn = _rms(h, g_ref[...]).astype(BF16)
    z = _dot(xn, win_ref[...])

    qf = z[:, C_Q:C_Q + D_GRP]
    kf = z[:, C_K:C_K + D_GRP]
    q_ref[0] = qf.astype(BF16)
    k_ref[0] = kf.astype(BF16)
    vt_ref[0] = z[:, C_V:C_V + D_GRP].T.astype(BF16)

    sq = jnp.concatenate([qf * qf, kf * kf], axis=1).astype(BF16)
    n2 = _dot(sq, e2_ref[...])

    f = z[:, C_F:C_F + LANES].T[0:SUBLANES, :] + bf_ref[:, 0:1]
    logf = (jnp.minimum(f, 0.0) - jnp.log(1.0 + jnp.exp(-jnp.abs(f)))) * LOG2E
    pos_l = lax.broadcasted_iota(jnp.int32, (SUBLANES, tm), 1)
    ct = logf
    shift = 1
    while shift < tm:
        ct = ct + jnp.where(pos_l >= shift, pltpu.roll(ct, shift, axis=1), 0.0)
        shift *= 2
    ct = ct + c_carry[:, 0:1]
    c_carry[...] = jnp.broadcast_to(ct[:, tm - 1:tm], (SUBLANES, LANES))
    crow_ref[0] = ct
    c = jnp.concatenate([ct, jnp.zeros((LANES - SUBLANES, tm), F32)], axis=0).T
    ccol_ref[0] = c

    nb = tm // STAT_ROWS
    c3 = c.reshape(nb, STAT_ROWS, LANES)
    bstat_ref[0, 0, 0:nb, :] = jnp.max(n2.reshape(nb, STAT_ROWS, LANES), axis=1)
    bstat_ref[0, 0, nb:2 * nb, :] = c3[:, 0, :]
    bstat_ref[0, 0, 2 * nb:3 * nb, :] = c3[:, STAT_ROWS - 1, :]

    u = z[:, C_CONF:C_CONF + D_GRP] * _sigmoid(z[:, C_CONF + D_GRP:C_CONF + 2 * D_GRP])
    u_ext[HALO_U:HALO_U + tm, :] = u
    n_a = (CONF_KERNEL - 1) // SUBLANES + 1
    rc = ROW_CHUNK
    for r0, l0 in [(r0, l0) for r0 in range(0, tm, CONV_ROWS)
                   for l0 in range(0, D_GRP, LANES)]:
        base = HALO_U + r0 - SUBLANES
        ls = slice(l0, l0 + LANES)
        ua = [u_ext[base - SUBLANES * a:base - SUBLANES * a + CONV_ROWS + SUBLANES, ls]
              for a in range(n_a)]
        total = None
        for r in range(SUBLANES):
            part = None
            for a in range(n_a):
                d = SUBLANES * a + r
                if d > CONF_KERNEL - 1:
                    continue
                kk = CONF_KERNEL - 1 - d
                term = wdw_ref[kk:kk + 1, ls] * ua[a]
                part = term if part is None else part + term
            if r:
                part = pltpu.roll(part, r, axis=0)
            part = part[SUBLANES:, :]
            total = part if total is None else total + part
        conv_buf[r0:r0 + CONV_ROWS, ls] = total
    u_ext[0:HALO_U, :] = u_ext[tm:tm + HALO_U, :]
    cv = conv_buf[...]
    mu = jnp.mean(cv, axis=-1, keepdims=True)
    xc = cv - mu
    ln = xc * lax.rsqrt(jnp.mean(xc * xc, axis=-1, keepdims=True) + EPS)
    ln = ln * lng_ref[...] + lnb_ref[...]
    act = ln * _sigmoid(ln)
    y3_ref[0, :, 0:D_GRP] = _dot(act.astype(BF16), wpw_ref[...]).astype(BF16)

    ch_ext[HALO_SC:HALO_SC + tm, :] = z[:, C_SCC:C_SCC + D_GRP] * z[:, C_SCH:C_SCH + D_GRP]
    for r0 in range(0, tm, rc):
        xs = ch_ext[r0:r0 + rc + HALO_SC, :]
        acc = wsc_ref[SC_KERNEL - 1:SC_KERNEL, :] * xs
        for d in range(1, SC_KERNEL):
            kk = SC_KERNEL - 1 - d
            acc = acc + wsc_ref[kk:kk + 1, :] * pltpu.roll(xs, d, axis=0)
        sc_buf[r0:r0 + rc, :] = acc[HALO_SC:, :]
    ch_ext[0:HALO_SC, :] = ch_ext[tm:tm + HALO_SC, :]
    y3_ref[0, :, D_GRP:2 * D_GRP] = (z[:, C_SCB:C_SCB + D_GRP] * sc_buf[...]).astype(BF16)

    pv = z[:, C_POOL:C_POOL + D_GRP]
    pv_ext[HALO_P:HALO_P + tm, :] = pv
    lane = lax.broadcasted_iota(jnp.int32, (1, D_GRP), 1)
    grp = lane // (D_GRP // len(POOL_WINDOWS))
    for r0 in range(0, tm, rc):
        run = pv_ext[r0:r0 + rc + HALO_P, :]
        sel = None
        w = 1
        for gi, wg in enumerate(POOL_WINDOWS):
            while w < wg:
                run = run + pltpu.roll(run, w, axis=0)
                w *= 2
            sel = run if sel is None else jnp.where(grp >= gi, run, sel)
        pool_buf[r0:r0 + rc, :] = sel[HALO_P:, :]
    pv_ext[0:HALO_P, :] = pv_ext[tm:tm + HALO_P, :]
    wl = jnp.zeros((1, D_GRP), jnp.int32)
    for gi, wg in enumerate(POOL_WINDOWS):
        wl = jnp.where(grp == gi, wg, wl)
    pos = s * tm + lax.broadcasted_iota(jnp.int32, (tm, D_GRP), 0)
    cnt = jnp.minimum(pos + 1, wl).astype(F32)
    d = pool_buf[...] / cnt - pv
    yp = _dot(d.astype(BF16), wpool_ref[...]) * pscale_ref[...]
    y3_ref[0, :, 2 * D_GRP:3 * D_GRP] = yp.astype(BF16)


def _layer_spec(a, layer, **kw):
    nd = a.ndim - 1
    return pl.BlockSpec((None,) + a.shape[1:], lambda *_: (layer,) + (0,) * nd, **kw)


def _front(h, layer, g, win, bf, wdw, lng, lnb, wpw, wsc, wpool, pscale, e2):
    B, S, _ = h.shape
    tm = min(TM_FRONT, S)
    nb = tm // STAT_ROWS
    grid = (B, S // tm)
    tok = lambda w: pl.BlockSpec((1, tm, w), lambda b, s: (b, s, 0))
    out_shape = (
        jax.ShapeDtypeStruct((B, S, D_GRP), BF16),
        jax.ShapeDtypeStruct((B, S, D_GRP), BF16),
        jax.ShapeDtypeStruct((B, D_GRP, S), BF16),
        jax.ShapeDtypeStruct((B, S, 3 * D_GRP), BF16),
        jax.ShapeDtypeStruct((B, S, LANES), F32),
        jax.ShapeDtypeStruct((B, SUBLANES, S), F32),
        jax.ShapeDtypeStruct((B, S // tm, 3 * nb, LANES), F32),
    )
    out_specs = (
        tok(D_GRP), tok(D_GRP),
        pl.BlockSpec((1, D_GRP, tm), lambda b, s: (b, 0, s)),
        tok(3 * D_GRP), tok(LANES),
        pl.BlockSpec((1, SUBLANES, tm), lambda b, s: (b, 0, s)),
        pl.BlockSpec((1, 1, 3 * nb, LANES), lambda b, s: (b, s, 0, 0)),
    )
    stacked = (g, win, bf, wdw, lng, lnb, wpw, wsc, wpool, pscale)
    consts = stacked + (e2,)
    in_specs = ([tok(D_MODEL)] + [_layer_spec(a, layer) for a in stacked]
                + [pl.BlockSpec(e2.shape, lambda b, s: (0, 0))])
    return pl.pallas_call(
        functools.partial(_front_kernel, tm=tm),
        grid=grid, in_specs=in_specs, out_specs=out_specs, out_shape=out_shape,
        scratch_shapes=[
            pltpu.VMEM((HALO_U + tm, D_GRP), F32),
            pltpu.VMEM((HALO_SC + tm, D_GRP), F32),
            pltpu.VMEM((HALO_P + tm, D_GRP), F32),
            pltpu.VMEM((tm, D_GRP), F32),
            pltpu.VMEM((tm, D_GRP), F32),
            pltpu.VMEM((tm, D_GRP), F32),
            pltpu.VMEM((SUBLANES, LANES), F32),
        ],
        compiler_params=pltpu.CompilerParams(
            dimension_semantics=("arbitrary", "arbitrary"),
            vmem_limit_bytes=VMEM_LIMIT),
        name="front",
    )(h, *consts)


def _first_needed_block(bstat, tq):
    B, nt, nb3, _ = bstat.shape
    nb = nb3 // 3
    flat = lambda a: a.reshape(B, nt * nb, LANES)
    n2 = flat(bstat[:, :, 0:nb])
    cs = flat(bstat[:, :, nb:2 * nb])[..., 0:N_HEADS]
    ce = flat(bstat[:, :, 2 * nb:3 * nb])[..., 0:N_HEADS]
    r = tq // STAT_ROWS
    nq = nt * nb // r
    qn = jnp.sqrt(jnp.max(n2[..., 0:N_HEADS].reshape(B, nq, r, N_HEADS), axis=2))
    kn = jnp.sqrt(jnp.max(n2[..., N_HEADS:2 * N_HEADS], axis=1, keepdims=True))
    bound = 2.0 * NORM_SLACK * qn * kn + 1.0
    c_start = cs[:, ::r]
    c_end = ce[:, r - 1::r]
    gap = c_start[:, :, None, :] - c_end[:, None, :, :]
    ii = jnp.arange(nq)[:, None]
    jj = jnp.arange(nq)[None, :]
    skip = (gap + bound[:, :, None, :] <= -SKIP_LOG2) & (jj < ii)[None, :, :, None]
    first = jnp.min(jnp.where(skip, nq, jj[None, :, :, None]), axis=2)
    return jnp.min(first, axis=2).astype(jnp.int32)


def _attn_kernel(jlo_ref, q_ref, k_ref, vt_ref, ccol_ref, crow_ref, o_ref, *, tq):
    b = pl.program_id(0)
    qi = pl.program_id(1)
    qs = pl.multiple_of(qi * tq, tq)
    lane = lax.broadcasted_iota(jnp.int32, (1, LANES), 1)
    nt = (((1,), (1,)), ((), ()))
    hpp = LANES // HEAD_DIM
    n_tiles = D_GRP // LANES
    hcols = lambda a, head: a[:, head * tq:(head + 1) * tq]
    hrows = lambda a, head: a[head * HEAD_DIM:(head + 1) * HEAD_DIM, :]

    q_tiles = []
    for t in range(n_tiles):
        q_pair = q_ref[0, :, t * LANES:(t + 1) * LANES]
        q_tiles.append(jnp.concatenate(
            [jnp.where((lane // HEAD_DIM) == hh, q_pair, jnp.zeros_like(q_pair))
             for hh in range(hpp)], axis=0))
    c_refs = [crow_ref[0, head:head + 1, pl.ds(qs, LANES)][:, 0:1]
              for head in range(N_HEADS)]

    def scores(starts, prev_live=None):
        parts = []
        for t in range(n_tiles):
            kb = jnp.concatenate([k_ref[0, pl.ds(ks, tq), t * LANES:(t + 1) * LANES]
                                  for ks in starts], axis=0)
            s2 = lax.dot_general(kb, q_tiles[t], nt, preferred_element_type=F32)
            for hh in range(hpp):
                head = t * hpp + hh
                cks = [ccol_ref[0, pl.ds(ks, tq), head:head + 1] - c_refs[head]
                       for ks in starts]
                if prev_live is not None:
                    cks[0] = jnp.where(prev_live, cks[0], jnp.inf)
                parts.append(hcols(s2, hh) - jnp.concatenate(cks, axis=0))
        return jnp.concatenate(parts, axis=1)

    def weighted_values(pb, starts):
        outs = []
        for head in range(N_HEADS):
            vtb = jnp.concatenate(
                [vt_ref[0, head * HEAD_DIM:(head + 1) * HEAD_DIM, pl.ds(ks, tq)]
                 for ks in starts], axis=1)
            outs.append(_dot(vtb, hcols(pb, head)))
        return outs

    prev_start = pl.multiple_of(jnp.maximum(qi - 1, 0) * tq, tq)
    st = scores([prev_start, qs], prev_live=qi > 0)
    key_i = lax.broadcasted_iota(jnp.int32, (tq, N_HEADS * tq), 0)
    qry_i = lax.broadcasted_iota(jnp.int32, (tq, N_HEADS * tq), 1) % tq
    st = jnp.concatenate([st[0:tq], jnp.where(key_i <= qry_i, st[tq:], -jnp.inf)], axis=0)
    m = jnp.max(st, axis=0, keepdims=True)
    p = jnp.exp2(st - m)
    l = jnp.sum(p, axis=0, keepdims=True)
    acc = jnp.concatenate(weighted_values(p.astype(BF16), [prev_start, qs]), axis=0)

    def block(j, carry):
        m, l, acc = carry
        ks = pl.multiple_of(j * tq, tq)
        st = scores([ks])
        m_new = jnp.maximum(m, jnp.max(st, axis=0, keepdims=True))
        alpha = jnp.exp2(m - m_new)
        p = jnp.exp2(st - m_new)
        l = alpha * l + jnp.sum(p, axis=0, keepdims=True)
        pv = weighted_values(p.astype(BF16), [ks])
        accs = [hcols(alpha, head) * hrows(acc, head) + pv[head]
                for head in range(N_HEADS)]
        return m_new, l, jnp.concatenate(accs, axis=0)

    _, l, acc = lax.fori_loop(jlo_ref[b, qi], qi - 1, block, (m, l, acc))
    o_t = jnp.concatenate([hrows(acc, head) / hcols(l, head)
                           for head in range(N_HEADS)], axis=0)
    o_ref[0] = o_t.T.astype(BF16)


def _attention(jlo, q, k, vt, ccol, crow):
    B, S, _ = q.shape
    tq = min(TQ, S)
    grid_spec = pltpu.PrefetchScalarGridSpec(
        num_scalar_prefetch=1,
        grid=(B, S // tq),
        in_specs=[
            pl.BlockSpec((1, tq, D_GRP), lambda b, i, jlo: (b, i, 0)),
            pl.BlockSpec((1, S, D_GRP), lambda b, i, jlo: (b, 0, 0)),
            pl.BlockSpec((1, D_GRP, S), lambda b, i, jlo: (b, 0, 0)),
            pl.BlockSpec((1, S, LANES), lambda b, i, jlo: (b, 0, 0)),
            pl.BlockSpec((1, SUBLANES, S), lambda b, i, jlo: (b, 0, 0)),
        ],
        out_specs=pl.BlockSpec((1, tq, D_GRP), lambda b, i, jlo: (b, i, 0)),
    )
    return pl.pallas_call(
        functools.partial(_attn_kernel, tq=tq),
        grid_spec=grid_spec,
        out_shape=jax.ShapeDtypeStruct((B, S, D_GRP), BF16),
        compiler_params=pltpu.CompilerParams(
            dimension_semantics=("arbitrary", "arbitrary"),
            vmem_limit_bytes=VMEM_LIMIT),
        name="attn",
    )(jlo, q, k, vt, ccol, crow)


def _back_kernel(h_ref, y3_ref, ya_ref, p_ref, gains_ref, wo_ref,
                 wup_ref, wdn_ref, wgate_ref, wproj_ref, o_ref):
    g_mix_post = gains_ref[0:1, :]
    g_mlp_pre = gains_ref[1:2, :]
    g_mlp_post = gains_ref[2:3, :]
    g_ple_pre = gains_ref[3:4, :]
    g_ple_post = gains_ref[4:5, :]

    h = h_ref[0]
    y = jnp.concatenate([y3_ref[0], ya_ref[0]], axis=1)
    h = h + _rms(_dot(y, wo_ref[...]), g_mix_post)

    r = _inv_rms(h)
    up = jnp.maximum(_dot((h * g_mlp_pre).astype(BF16), wup_ref[...]), 0.0)
    f = _dot((up * up).astype(BF16), wdn_ref[...])
    r2 = r * r
    f_scale = r2 * lax.rsqrt(r2 * r2 * jnp.mean(f * f, axis=-1, keepdims=True) + EPS)
    h = h + f * f_scale * g_mlp_post

    r = _inv_rms(h)
    gate = _sigmoid(r * _dot((h * g_ple_pre).astype(BF16), wgate_ref[...]))
    e = _dot(p_ref[0, 0].astype(BF16), wproj_ref[...]) * gate
    o_ref[0] = h + _rms(e, g_ple_post)


def _back(h, y3, ya, p, layer, gains, wo, wup, wdn, wgate, wproj):
    B, S, _ = h.shape
    tm = min(TM_BACK, S)
    grid = (B, S // tm)
    tok = lambda w: pl.BlockSpec((1, tm, w), lambda b, s: (b, s, 0))
    weights = (gains, wo, wup, wdn, wgate, wproj)
    in_specs = [
        tok(D_MODEL), tok(3 * D_GRP), tok(D_GRP),
        pl.BlockSpec((1, 1, tm, D_PLE), lambda b, s: (layer, b, s, 0)),
    ] + [_layer_spec(a, layer, pipeline_mode=pl.Buffered(1)) for a in weights]
    return pl.pallas_call(
        _back_kernel,
        grid=grid, in_specs=in_specs, out_specs=tok(D_MODEL),
        out_shape=jax.ShapeDtypeStruct((B, S, D_MODEL), F32),
        compiler_params=pltpu.CompilerParams(
            dimension_semantics=("arbitrary", "arbitrary"),
            vmem_limit_bytes=VMEM_LIMIT),
        name="back",
    )(h, y3, ya, p, *weights)


def _permute_w_in(w_in):
    o = 0
    conf = w_in[..., o:o + 512]; o += 512
    q = w_in[..., o:o + 256] * (HEAD_DIM ** -0.5 * LOG2E); o += 256
    k = w_in[..., o:o + 256]; o += 256
    v = w_in[..., o:o + 256]; o += 256
    f = w_in[..., o:o + N_HEADS]; o += N_HEADS
    rest = w_in[..., o:]
    f = jnp.pad(f, ((0, 0), (0, 0), (0, LANES - N_HEADS)))
    return jnp.concatenate([conf, q, k, v, rest, f], axis=-1).astype(BF16)


def _block_diag(w_pool):
    L, G, C, _ = w_pool.shape
    eye = jnp.eye(G, dtype=w_pool.dtype)
    return jnp.einsum("lgcd,gh->lgchd", w_pool, eye).reshape(L, G * C, G * C)


def _head_indicator():
    e = np.zeros((2 * D_GRP, LANES), np.float32)
    for d in range(2 * D_GRP):
        e[d, d // HEAD_DIM] = 1.0
    return jnp.asarray(e, BF16)


def kernel(x, p, g_mix_pre, w_in, b_forget, w_conf_dw, conf_ln_g, conf_ln_b, w_conf_pw, w_sc, w_pool, pool_scale, w_out, g_mix_post, g_mlp_pre, w_up, w_down, g_mlp_post, g_ple_pre, w_ple_gate, w_ple_proj, g_ple_post):
    L = w_in.shape[0]
    S = x.shape[1]
    win = _permute_w_in(w_in)
    bf = jnp.broadcast_to(jnp.pad(b_forget, ((0, 0), (0, SUBLANES - N_HEADS)))[:, :, None],
                          (L, SUBLANES, LANES))
    wpw = w_conf_pw.astype(BF16)
    wpool = _block_diag(w_pool).astype(BF16)
    wo = jnp.concatenate([w_out[:, 0:256], w_out[:, 512:1024], w_out[:, 256:512]],
                         axis=1).astype(BF16)
    wup = w_up.astype(BF16)
    wdn = w_down.astype(BF16)
    wgate = w_ple_gate.astype(BF16)
    wproj = w_ple_proj.astype(BF16)
    zero = jnp.zeros_like(g_mix_post)
    gains = jnp.stack([g_mix_post, g_mlp_pre, g_mlp_post, g_ple_pre, g_ple_post,
                       zero, zero, zero], axis=1)
    e2 = _head_indicator()

    row = lambda a: a[:, None, :]
    h = x
    for i in range(L):
        q, k, vt, y3, ccol, crow, bstat = _front(
            h, i, row(g_mix_pre), win, bf, w_conf_dw, row(conf_ln_g), row(conf_ln_b),
            wpw, w_sc, wpool, row(pool_scale), e2)
        jlo = _first_needed_block(bstat, min(TQ, S))
        ya = _attention(jlo, q, k, vt, ccol, crow)
        h = _back(h, y3, ya, p, i, gains, wo, wup, wdn, wgate, wproj)
    return h
```

```python
import functools

import jax
import jax.numpy as jnp
import numpy as np
from jax import lax
from jax.experimental import pallas as pl
from jax.experimental.pallas import tpu as pltpu

F32 = jnp.float32
BF16 = jnp.bfloat16

D_MODEL = 1024
D_GRP = 256
HEAD_DIM = 64
N_HEADS = D_GRP // HEAD_DIM
CONF_KERNEL = 31
SC_KERNEL = 3
POOL_WINDOWS = (2, 4, 8, 16)
D_FF = 4 * D_MODEL
D_PLE = 256
EPS = 1e-6
LANES = 128
SUBLANES = 8
LOG2E = 1.4426950408889634

C_CONF = 0
C_Q = 512
C_K = 768
C_V = 1024
C_SCH = 1280
C_SCB = 1536
C_SCC = 1792
C_POOL = 2048
C_F = 2304
D_Z = C_F + LANES

HALO_U = 32
HALO_SC = 8
HALO_P = 16
ROW_CHUNK = 64
CONV_ROWS = 64
STAT_ROWS = 128

TM_FRONT = 1024
TM_BACK = 1024
TQ = 256
ATTN_BLOCKS_PER_STEP = 4
VMEM_LIMIT = 56 * 1024 * 1024
BACK_VMEM_LIMIT = 60 * 1024 * 1024

SKIP_LOG2 = 104.0 * LOG2E + 1.0
NORM_SLACK = 1.02


def _inv_rms(x):
    return lax.rsqrt(jnp.mean(x * x, axis=-1, keepdims=True) + EPS)


def _rms(x, g):
    return x * _inv_rms(x) * g


def _sigmoid(x):
    return 1.0 / (1.0 + jnp.exp(-x))


def _dot(a, b):
    return jnp.dot(a, b, preferred_element_type=F32)


def _front_kernel(h_ref, g_ref, win_ref, bf_ref, wdw_ref, lng_ref, lnb_ref,
                  wpw_ref, wsc_ref, wpool_ref, pscale_ref, e2_ref,
                  q_ref, k_ref, vt_ref, y3_ref, ccol_ref, crow_ref, bstat_ref,
                  u_ext, ch_ext, pv_ext, conv_buf, sc_buf, pool_buf, c_carry, *, tm):
    s = pl.program_id(1)

    @pl.when(s == 0)
    def _():
        u_ext[0:HALO_U, :] = jnp.zeros((HALO_U, D_GRP), F32)
        ch_ext[0:HALO_SC, :] = jnp.zeros((HALO_SC, D_GRP), F32)
        pv_ext[0:HALO_P, :] = jnp.zeros((HALO_P, D_GRP), F32)
        c_carry[...] = jnp.zeros((SUBLANES, LANES), F32)

    h = h_ref[0]
    xn = _rms(h, g_ref[...]).astype(BF16)
    z = _dot(xn, win_ref[...])

    qf = z[:, C_Q:C_Q + D_GRP]
    kf = z[:, C_K:C_K + D_GRP]
    q_ref[0] = qf.astype(BF16)
    k_ref[0] = kf.astype(BF16)
    vt_ref[0] = z[:, C_V:C_V + D_GRP].T.astype(BF16)

    sq = jnp.concatenate([qf * qf, kf * kf], axis=1).astype(BF16)
    n2 = _dot(sq, e2_ref[...])

    f = z[:, C_F:C_F + LANES].T[0:SUBLANES, :] + bf_ref[:, 0:1]
    logf = (jnp.minimum(f, 0.0) - jnp.log(1.0 + jnp.exp(-jnp.abs(f)))) * LOG2E
    pos_l = lax.broadcasted_iota(jnp.int32, (SUBLANES, tm), 1)
    ct = logf
    shift = 1
    while shift < tm:
        ct = ct + jnp.where(pos_l >= shift, pltpu.roll(ct, shift, axis=1), 0.0)
        shift *= 2
    ct = ct + c_carry[:, 0:1]
    c_carry[...] = jnp.broadcast_to(ct[:, tm - 1:tm], (SUBLANES, LANES))
    crow_ref[0] = ct
    c = jnp.concatenate([ct, jnp.zeros((LANES - SUBLANES, tm), F32)], axis=0).T
    ccol_ref[0] = c

    nb = tm // STAT_ROWS
    c3 = c.reshape(nb, STAT_ROWS, LANES)
    bstat_ref[0, 0, 0:nb, :] = jnp.max(n2.reshape(nb, STAT_ROWS, LANES), axis=1)
    bstat_ref[0, 0, nb:2 * nb, :] = c3[:, 0, :]
    bstat_ref[0, 0, 2 * nb:3 * nb, :] = c3[:, STAT_ROWS - 1, :]

    u = z[:, C_CONF:C_CONF + D_GRP] * _sigmoid(z[:, C_CONF + D_GRP:C_CONF + 2 * D_GRP])
    u_ext[HALO_U:HALO_U + tm, :] = u
    n_a = (CONF_KERNEL - 1) // SUBLANES + 1
    rc = ROW_CHUNK
    for r0, l0 in [(r0, l0) for r0 in range(0, tm, CONV_ROWS)
                   for l0 in range(0, D_GRP, LANES)]:
        base = HALO_U + r0 - SUBLANES
        ls = slice(l0, l0 + LANES)
        ua = [u_ext[base - SUBLANES * a:base - SUBLANES * a + CONV_ROWS + SUBLANES, ls]
              for a in range(n_a)]
        total = None
        for r in range(SUBLANES):
            part = None
            for a in range(n_a):
                d = SUBLANES * a + r
                if d > CONF_KERNEL - 1:
                    continue
                kk = CONF_KERNEL - 1 - d
                term = wdw_ref[kk:kk + 1, ls] * ua[a]
                part = term if part is None else part + term
            if r:
                part = pltpu.roll(part, r, axis=0)
            part = part[SUBLANES:, :]
            total = part if total is None else total + part
        conv_buf[r0:r0 + CONV_ROWS, ls] = total
    u_ext[0:HALO_U, :] = u_ext[tm:tm + HALO_U, :]
    cv = conv_buf[...]
    mu = jnp.mean(cv, axis=-1, keepdims=True)
    xc = cv - mu
    ln = xc * lax.rsqrt(jnp.mean(xc * xc, axis=-1, keepdims=True) + EPS)
    ln = ln * lng_ref[...] + lnb_ref[...]
    act = ln * _sigmoid(ln)
    y3_ref[0, :, 0:D_GRP] = _dot(act.astype(BF16), wpw_ref[...]).astype(BF16)

    ch_ext[HALO_SC:HALO_SC + tm, :] = z[:, C_SCC:C_SCC + D_GRP] * z[:, C_SCH:C_SCH + D_GRP]
    for r0 in range(0, tm, rc):
        xs = ch_ext[r0:r0 + rc + HALO_SC, :]
        acc = wsc_ref[SC_KERNEL - 1:SC_KERNEL, :] * xs
        for d in range(1, SC_KERNEL):
            kk = SC_KERNEL - 1 - d
            acc = acc + wsc_ref[kk:kk + 1, :] * pltpu.roll(xs, d, axis=0)
        sc_buf[r0:r0 + rc, :] = acc[HALO_SC:, :]
    ch_ext[0:HALO_SC, :] = ch_ext[tm:tm + HALO_SC, :]
    y3_ref[0, :, D_GRP:2 * D_GRP] = (z[:, C_SCB:C_SCB + D_GRP] * sc_buf[...]).astype(BF16)

    pv = z[:, C_POOL:C_POOL + D_GRP]
    pv_ext[HALO_P:HALO_P + tm, :] = pv
    lane = lax.broadcasted_iota(jnp.int32, (1, D_GRP), 1)
    grp = lane // (D_GRP // len(POOL_WINDOWS))
    for r0 in range(0, tm, rc):
        run = pv_ext[r0:r0 + rc + HALO_P, :]
        sel = None
        w = 1
        for gi, wg in enumerate(POOL_WINDOWS):
            while w < wg:
                run = run + pltpu.roll(run, w, axis=0)
                w *= 2
            sel = run if sel is None else jnp.where(grp >= gi, run, sel)
        pool_buf[r0:r0 + rc, :] = sel[HALO_P:, :]
    pv_ext[0:HALO_P, :] = pv_ext[tm:tm + HALO_P, :]
    wl = jnp.zeros((1, D_GRP), jnp.int32)
    for gi, wg in enumerate(POOL_WINDOWS):
        wl = jnp.where(grp == gi, wg, wl)
    pos = s * tm + lax.broadcasted_iota(jnp.int32, (tm, D_GRP), 0)
    cnt = jnp.minimum(pos + 1, wl).astype(F32)
    d = pool_buf[...] / cnt - pv
    yp = _dot(d.astype(BF16), wpool_ref[...]) * pscale_ref[...]
    y3_ref[0, :, 2 * D_GRP:3 * D_GRP] = yp.astype(BF16)


def _layer_spec(a, layer, **kw):
    nd = a.ndim - 1
    return pl.BlockSpec((None,) + a.shape[1:], lambda *_: (layer,) + (0,) * nd, **kw)


def _front(h, layer, g, win, bf, wdw, lng, lnb, wpw, wsc, wpool, pscale, e2):
    B, S, _ = h.shape
    tm = min(TM_FRONT, S)
    nb = tm // STAT_ROWS
    grid = (B, S // tm)
    tok = lambda w: pl.BlockSpec((1, tm, w), lambda b, s: (b, s, 0))
    out_shape = (
        jax.ShapeDtypeStruct((B, S, D_GRP), BF16),
        jax.ShapeDtypeStruct((B, S, D_GRP), BF16),
        jax.ShapeDtypeStruct((B, D_GRP, S), BF16),
        jax.ShapeDtypeStruct((B, S, 3 * D_GRP), BF16),
        jax.ShapeDtypeStruct((B, S, LANES), F32),
        jax.ShapeDtypeStruct((B, SUBLANES, S), F32),
        jax.ShapeDtypeStruct((B, S // tm, 3 * nb, LANES), F32),
    )
    out_specs = (
        tok(D_GRP), tok(D_GRP),
        pl.BlockSpec((1, D_GRP, tm), lambda b, s: (b, 0, s)),
        tok(3 * D_GRP), tok(LANES),
        pl.BlockSpec((1, SUBLANES, tm), lambda b, s: (b, 0, s)),
        pl.BlockSpec((1, 1, 3 * nb, LANES), lambda b, s: (b, s, 0, 0)),
    )
    stacked = (g, win, bf, wdw, lng, lnb, wpw, wsc, wpool, pscale)
    consts = stacked + (e2,)
    in_specs = ([tok(D_MODEL)] + [_layer_spec(a, layer) for a in stacked]
                + [pl.BlockSpec(e2.shape, lambda b, s: (0, 0))])
    return pl.pallas_call(
        functools.partial(_front_kernel, tm=tm),
        grid=grid, in_specs=in_specs, out_specs=out_specs, out_shape=out_shape,
        scratch_shapes=[
            pltpu.VMEM((HALO_U + tm, D_GRP), F32),
            pltpu.VMEM((HALO_SC + tm, D_GRP), F32),
            pltpu.VMEM((HALO_P + tm, D_GRP), F32),
            pltpu.VMEM((tm, D_GRP), F32),
            pltpu.VMEM((tm, D_GRP), F32),
            pltpu.VMEM((tm, D_GRP), F32),
            pltpu.VMEM((SUBLANES, LANES), F32),
        ],
        compiler_params=pltpu.CompilerParams(
            dimension_semantics=("arbitrary", "arbitrary"),
            vmem_limit_bytes=VMEM_LIMIT),
        name="front",
    )(h, *consts)


def _first_needed_block(bstat, tq):
    B, nt, nb3, _ = bstat.shape
    nb = nb3 // 3
    flat = lambda a: a.reshape(B, nt * nb, LANES)
    n2 = flat(bstat[:, :, 0:nb])
    cs = flat(bstat[:, :, nb:2 * nb])[..., 0:N_HEADS]
    ce = flat(bstat[:, :, 2 * nb:3 * nb])[..., 0:N_HEADS]
    r = tq // STAT_ROWS
    nq = nt * nb // r
    qn = jnp.sqrt(jnp.max(n2[..., 0:N_HEADS].reshape(B, nq, r, N_HEADS), axis=2))
    kn = jnp.sqrt(jnp.max(n2[..., N_HEADS:2 * N_HEADS], axis=1, keepdims=True))
    bound = 2.0 * NORM_SLACK * qn * kn + 1.0
    c_start = cs[:, ::r]
    c_end = ce[:, r - 1::r]
    gap = c_start[:, :, None, :] - c_end[:, None, :, :]
    ii = jnp.arange(nq)[:, None]
    jj = jnp.arange(nq)[None, :]
    skip = (gap + bound[:, :, None, :] <= -SKIP_LOG2) & (jj < ii)[None, :, :, None]
    first = jnp.min(jnp.where(skip, nq, jj[None, :, :, None]), axis=2)
    return jnp.min(first, axis=2).astype(jnp.int32)


def _attn_kernel(jlo_ref, q_ref, k_ref, vt_ref, ccol_ref, crow_ref, o_ref, *, tq):
    b = pl.program_id(0)
    n = q_ref.shape[1] // tq
    first = pl.program_id(1) * n

    def body(t, carry):
        _attn_query_block(b, first + t, t, jlo_ref, q_ref, k_ref, vt_ref, ccol_ref,
                          crow_ref, o_ref, tq)
        return carry

    lax.fori_loop(0, n, body, 0)


def _attn_query_block(b, qi, t, jlo_ref, q_ref, k_ref, vt_ref, ccol_ref, crow_ref, o_ref, tq):
    qs = pl.multiple_of(qi * tq, tq)
    q_rows = pl.ds(pl.multiple_of(t * tq, tq), tq)
    lane = lax.broadcasted_iota(jnp.int32, (1, LANES), 1)
    nt = (((1,), (1,)), ((), ()))
    hpp = LANES // HEAD_DIM
    n_tiles = D_GRP // LANES
    hcols = lambda a, head: a[:, head * tq:(head + 1) * tq]
    hrows = lambda a, head: a[head * HEAD_DIM:(head + 1) * HEAD_DIM, :]

    q_tiles = []
    for t in range(n_tiles):
        q_pair = q_ref[0, q_rows, t * LANES:(t + 1) * LANES]
        q_tiles.append(jnp.concatenate(
            [jnp.where((lane // HEAD_DIM) == hh, q_pair, jnp.zeros_like(q_pair))
             for hh in range(hpp)], axis=0))
    c_refs = [crow_ref[0, head:head + 1, pl.ds(qs, LANES)][:, 0:1]
              for head in range(N_HEADS)]

    def scores(starts, prev_live=None):
        parts = []
        for t in range(n_tiles):
            kb = jnp.concatenate([k_ref[0, pl.ds(ks, tq), t * LANES:(t + 1) * LANES]
                                  for ks in starts], axis=0)
            s2 = lax.dot_general(kb, q_tiles[t], nt, preferred_element_type=F32)
            for hh in range(hpp):
                head = t * hpp + hh
                cks = [ccol_ref[0, pl.ds(ks, tq), head:head + 1] - c_refs[head]
                       for ks in starts]
                if prev_live is not None:
                    cks[0] = jnp.where(prev_live, cks[0], jnp.inf)
                parts.append(hcols(s2, hh) - jnp.concatenate(cks, axis=0))
        return jnp.concatenate(parts, axis=1)

    def weighted_values(pb, starts):
        outs = []
        for head in range(N_HEADS):
            vtb = jnp.concatenate(
                [vt_ref[0, head * HEAD_DIM:(head + 1) * HEAD_DIM, pl.ds(ks, tq)]
                 for ks in starts], axis=1)
            outs.append(_dot(vtb, hcols(pb, head)))
        return outs

    prev_start = pl.multiple_of(jnp.maximum(qi - 1, 0) * tq, tq)
    st = scores([prev_start, qs], prev_live=qi > 0)
    key_i = lax.broadcasted_iota(jnp.int32, (tq, N_HEADS * tq), 0)
    qry_i = lax.broadcasted_iota(jnp.int32, (tq, N_HEADS * tq), 1) % tq
    st = jnp.concatenate([st[0:tq], jnp.where(key_i <= qry_i, st[tq:], -jnp.inf)], axis=0)
    m = jnp.max(st, axis=0, keepdims=True)
    p = jnp.exp2(st - m)
    l = jnp.sum(p, axis=0, keepdims=True)
    acc = jnp.concatenate(weighted_values(p.astype(BF16), [prev_start, qs]), axis=0)

    def block(j, carry):
        m, l, acc = carry
        ks = pl.multiple_of(j * tq, tq)
        st = scores([ks])
        m_new = jnp.maximum(m, jnp.max(st, axis=0, keepdims=True))
        alpha = jnp.exp2(m - m_new)
        p = jnp.exp2(st - m_new)
        l = alpha * l + jnp.sum(p, axis=0, keepdims=True)
        pv = weighted_values(p.astype(BF16), [ks])
        accs = [hcols(alpha, head) * hrows(acc, head) + pv[head]
                for head in range(N_HEADS)]
        return m_new, l, jnp.concatenate(accs, axis=0)

    _, l, acc = lax.fori_loop(jlo_ref[b, qi], qi - 1, block, (m, l, acc))
    o_t = jnp.concatenate([hrows(acc, head) / hcols(l, head)
                           for head in range(N_HEADS)], axis=0)
    o_ref[0, q_rows, :] = o_t.T.astype(BF16)


def _attention(jlo, q, k, vt, ccol, crow):
    B, S, _ = q.shape
    tq = min(TQ, S)
    ts = min(TQ * ATTN_BLOCKS_PER_STEP, S)
    grid_spec = pltpu.PrefetchScalarGridSpec(
        num_scalar_prefetch=1,
        grid=(B, S // ts),
        in_specs=[
            pl.BlockSpec((1, ts, D_GRP), lambda b, i, jlo: (b, i, 0)),
            pl.BlockSpec((1, S, D_GRP), lambda b, i, jlo: (b, 0, 0)),
            pl.BlockSpec((1, D_GRP, S), lambda b, i, jlo: (b, 0, 0)),
            pl.BlockSpec((1, S, LANES), lambda b, i, jlo: (b, 0, 0)),
            pl.BlockSpec((1, SUBLANES, S), lambda b, i, jlo: (b, 0, 0)),
        ],
        out_specs=pl.BlockSpec((1, ts, D_GRP), lambda b, i, jlo: (b, i, 0)),
    )
    return pl.pallas_call(
        functools.partial(_attn_kernel, tq=tq),
        grid_spec=grid_spec,
        out_shape=jax.ShapeDtypeStruct((B, S, D_GRP), BF16),
        compiler_params=pltpu.CompilerParams(
            dimension_semantics=("arbitrary", "arbitrary"),
            vmem_limit_bytes=VMEM_LIMIT),
        name="attn",
    )(jlo, q, k, vt, ccol, crow)


def _back_kernel(h_ref, y3_ref, ya_ref, p_ref, gains_ref, wo_ref,
                 wup_ref, wdn_ref, wgate_ref, wproj_ref, o_ref):
    g_mix_post = gains_ref[0:1, :]
    g_mlp_pre = gains_ref[1:2, :]
    g_mlp_post = gains_ref[2:3, :]
    g_ple_pre = gains_ref[3:4, :]
    g_ple_post = gains_ref[4:5, :]

    h = h_ref[0]
    y = jnp.concatenate([y3_ref[0], ya_ref[0]], axis=1)
    h = h + _rms(_dot(y, wo_ref[...]), g_mix_post)

    r = _inv_rms(h)
    up = jnp.maximum(_dot((h * g_mlp_pre).astype(BF16), wup_ref[...]), 0.0)
    f = _dot((up * up).astype(BF16), wdn_ref[...])
    r2 = r * r
    f_scale = r2 * lax.rsqrt(r2 * r2 * jnp.mean(f * f, axis=-1, keepdims=True) + EPS)
    h = h + f * f_scale * g_mlp_post

    r = _inv_rms(h)
    gate = _sigmoid(r * _dot((h * g_ple_pre).astype(BF16), wgate_ref[...]))
    e = _dot(p_ref[0, 0].astype(BF16), wproj_ref[...]) * gate
    o_ref[0] = h + _rms(e, g_ple_post)


def _back(h, y3, ya, p, layer, gains, wo, wup, wdn, wgate, wproj):
    B, S, _ = h.shape
    tm = min(TM_BACK, S)
    grid = (B, S // tm)
    tok = lambda w: pl.BlockSpec((1, tm, w), lambda b, s: (b, s, 0))
    weights = (gains, wo, wup, wdn, wgate, wproj)
    in_specs = [
        tok(D_MODEL), tok(3 * D_GRP), tok(D_GRP),
        pl.BlockSpec((1, 1, tm, D_PLE), lambda b, s: (layer, b, s, 0)),
    ] + [_layer_spec(a, layer, pipeline_mode=pl.Buffered(1)) for a in weights]
    return pl.pallas_call(
        _back_kernel,
        grid=grid, in_specs=in_specs, out_specs=tok(D_MODEL),
        out_shape=jax.ShapeDtypeStruct((B, S, D_MODEL), F32),
        compiler_params=pltpu.CompilerParams(
            dimension_semantics=("arbitrary", "arbitrary"),
            vmem_limit_bytes=BACK_VMEM_LIMIT),
        name="back",
    )(h, y3, ya, p, *weights)


def _permute_w_in(w_in):
    o = 0
    conf = w_in[..., o:o + 512]; o += 512
    q = w_in[..., o:o + 256] * (HEAD_DIM ** -0.5 * LOG2E); o += 256
    kv = w_in[..., o:o + 512]; o += 512
    f = w_in[..., o:o + N_HEADS]; o += N_HEADS
    rest = w_in[..., o:]
    f = jnp.pad(f, ((0, 0), (0, 0), (0, LANES - N_HEADS)))
    return jnp.concatenate([a.astype(BF16) for a in (conf, q, kv, rest, f)], axis=-1)


def _block_diag(w_pool):
    L, G, C, _ = w_pool.shape
    eye = jnp.eye(G, dtype=w_pool.dtype)
    return jnp.einsum("lgcd,gh->lgchd", w_pool, eye).reshape(L, G * C, G * C)


def _head_indicator():
    e = np.zeros((2 * D_GRP, LANES), np.float32)
    for d in range(2 * D_GRP):
        e[d, d // HEAD_DIM] = 1.0
    return jnp.asarray(e, BF16)


def kernel(x, p, g_mix_pre, w_in, b_forget, w_conf_dw, conf_ln_g, conf_ln_b, w_conf_pw, w_sc, w_pool, pool_scale, w_out, g_mix_post, g_mlp_pre, w_up, w_down, g_mlp_post, g_ple_pre, w_ple_gate, w_ple_proj, g_ple_post):
    L = w_in.shape[0]
    S = x.shape[1]
    win = _permute_w_in(w_in)
    bf = jnp.broadcast_to(jnp.pad(b_forget, ((0, 0), (0, SUBLANES - N_HEADS)))[:, :, None],
                          (L, SUBLANES, LANES))
    wpw = w_conf_pw.astype(BF16)
    wpool = _block_diag(w_pool).astype(BF16)
    wo = jnp.concatenate([w_out[:, 0:256], w_out[:, 512:1024], w_out[:, 256:512]],
                         axis=1).astype(BF16)
    wup = w_up.astype(BF16)
    wdn = w_down.astype(BF16)
    wgate = w_ple_gate.astype(BF16)
    wproj = w_ple_proj.astype(BF16)
    zero = jnp.zeros_like(g_mix_post)
    gains = jnp.stack([g_mix_post, g_mlp_pre, g_mlp_post, g_ple_pre, g_ple_post,
                       zero, zero, zero], axis=1)
    e2 = _head_indicator()

    row = lambda a: a[:, None, :]
    h = x
    for i in range(L):
        q, k, vt, y3, ccol, crow, bstat = _front(
            h, i, row(g_mix_pre), win, bf, w_conf_dw, row(conf_ln_g), row(conf_ln_b),
            wpw, w_sc, wpool, row(pool_scale), e2)
        jlo = _first_needed_block(bstat, min(TQ, S))
        ya = _attention(jlo, q, k, vt, ccol, crow)
        h = _back(h, y3, ya, p, i, gains, wo, wup, wdn, wgate, wproj)
    return h
```

```python
import functools

import jax
import jax.numpy as jnp
import numpy as np
from jax import lax
from jax.experimental import pallas as pl
from jax.experimental.pallas import tpu as pltpu

F32 = jnp.float32
BF16 = jnp.bfloat16

D_MODEL = 1024
D_GRP = 256
HEAD_DIM = 64
N_HEADS = D_GRP // HEAD_DIM
CONF_KERNEL = 31
SC_KERNEL = 3
POOL_WINDOWS = (2, 4, 8, 16)
D_FF = 4 * D_MODEL
D_PLE = 256
EPS = 1e-6
LANES = 128
SUBLANES = 8
LOG2E = 1.4426950408889634

C_CONF = 0
C_Q = 512
C_K = 768
C_V = 1024
C_SCH = 1280
C_SCB = 1536
C_SCC = 1792
C_POOL = 2048
C_F = 2304
D_Z = C_F + LANES

HALO_U = 32
HALO_SC = 8
HALO_P = 16
ROW_CHUNK = 64
CONV_ROWS = 64
STAT_ROWS = 128

TM_FRONT = 1024
TM_BACK = 1024
TQ = 256
ATTN_BLOCKS_PER_STEP = 4
ATTN_ROWS = 64
VMEM_LIMIT = 56 * 1024 * 1024
BACK_VMEM_LIMIT = 60 * 1024 * 1024

SKIP_LOG2 = 104.0 * LOG2E + 1.0
NORM_SLACK = 1.02


def _inv_rms(x):
    return lax.rsqrt(jnp.mean(x * x, axis=-1, keepdims=True) + EPS)


def _rms(x, g):
    return x * _inv_rms(x) * g


def _sigmoid(x):
    return 1.0 / (1.0 + jnp.exp(-x))


def _dot(a, b):
    return jnp.dot(a, b, preferred_element_type=F32)


def _front_kernel(h_ref, g_ref, win_ref, bf_ref, wdw_ref, lng_ref, lnb_ref,
                  wpw_ref, wsc_ref, wpool_ref, pscale_ref, e2_ref,
                  q_ref, k_ref, vt_ref, y3_ref, ccol_ref, crow_ref, bstat_ref,
                  u_ext, ch_ext, pv_ext, conv_buf, sc_buf, pool_buf, c_carry, *, tm):
    s = pl.program_id(1)

    @pl.when(s == 0)
    def _():
        u_ext[0:HALO_U, :] = jnp.zeros((HALO_U, D_GRP), F32)
        ch_ext[0:HALO_SC, :] = jnp.zeros((HALO_SC, D_GRP), F32)
        pv_ext[0:HALO_P, :] = jnp.zeros((HALO_P, D_GRP), F32)
        c_carry[...] = jnp.zeros((SUBLANES, LANES), F32)

    h = h_ref[0]
    xn = _rms(h, g_ref[...]).astype(BF16)
    z = _dot(xn, win_ref[...])

    qf = z[:, C_Q:C_Q + D_GRP]
    kf = z[:, C_K:C_K + D_GRP]
    q_ref[0] = qf.astype(BF16)
    k_ref[0] = kf.astype(BF16)
    vt_ref[0] = z[:, C_V:C_V + D_GRP].T.astype(BF16)

    sq = jnp.concatenate([qf * qf, kf * kf], axis=1).astype(BF16)
    n2 = _dot(sq, e2_ref[...])

    f = z[:, C_F:C_F + LANES].T[0:SUBLANES, :] + bf_ref[:, 0:1]
    logf = (jnp.minimum(f, 0.0) - jnp.log(1.0 + jnp.exp(-jnp.abs(f)))) * LOG2E
    pos_l = lax.broadcasted_iota(jnp.int32, (SUBLANES, tm), 1)
    ct = logf
    shift = 1
    while shift < tm:
        ct = ct + jnp.where(pos_l >= shift, pltpu.roll(ct, shift, axis=1), 0.0)
        shift *= 2
    ct = ct + c_carry[:, 0:1]
    c_carry[...] = jnp.broadcast_to(ct[:, tm - 1:tm], (SUBLANES, LANES))
    crow_ref[0] = ct
    c = jnp.concatenate([ct, jnp.zeros((LANES - SUBLANES, tm), F32)], axis=0).T
    ccol_ref[0] = c

    nb = tm // STAT_ROWS
    c3 = c.reshape(nb, STAT_ROWS, LANES)
    bstat_ref[0, 0, 0:nb, :] = jnp.max(n2.reshape(nb, STAT_ROWS, LANES), axis=1)
    bstat_ref[0, 0, nb:2 * nb, :] = c3[:, 0, :]
    bstat_ref[0, 0, 2 * nb:3 * nb, :] = c3[:, STAT_ROWS - 1, :]

    u = z[:, C_CONF:C_CONF + D_GRP] * _sigmoid(z[:, C_CONF + D_GRP:C_CONF + 2 * D_GRP])
    u_ext[HALO_U:HALO_U + tm, :] = u
    n_a = (CONF_KERNEL - 1) // SUBLANES + 1
    rc = ROW_CHUNK
    for r0, l0 in [(r0, l0) for r0 in range(0, tm, CONV_ROWS)
                   for l0 in range(0, D_GRP, LANES)]:
        base = HALO_U + r0 - SUBLANES
        ls = slice(l0, l0 + LANES)
        ua = [u_ext[base - SUBLANES * a:base - SUBLANES * a + CONV_ROWS + SUBLANES, ls]
              for a in range(n_a)]
        total = None
        for r in range(SUBLANES):
            part = None
            for a in range(n_a):
                d = SUBLANES * a + r
                if d > CONF_KERNEL - 1:
                    continue
                kk = CONF_KERNEL - 1 - d
                term = wdw_ref[kk:kk + 1, ls] * ua[a]
                part = term if part is None else part + term
            if r:
                part = pltpu.roll(part, r, axis=0)
            part = part[SUBLANES:, :]
            total = part if total is None else total + part
        conv_buf[r0:r0 + CONV_ROWS, ls] = total
    u_ext[0:HALO_U, :] = u_ext[tm:tm + HALO_U, :]
    cv = conv_buf[...]
    mu = jnp.mean(cv, axis=-1, keepdims=True)
    xc = cv - mu
    ln = xc * lax.rsqrt(jnp.mean(xc * xc, axis=-1, keepdims=True) + EPS)
    ln = ln * lng_ref[...] + lnb_ref[...]
    act = ln * _sigmoid(ln)
    y3_ref[0, :, 0:D_GRP] = _dot(act.astype(BF16), wpw_ref[...]).astype(BF16)

    ch_ext[HALO_SC:HALO_SC + tm, :] = z[:, C_SCC:C_SCC + D_GRP] * z[:, C_SCH:C_SCH + D_GRP]
    for r0 in range(0, tm, rc):
        xs = ch_ext[r0:r0 + rc + HALO_SC, :]
        acc = wsc_ref[SC_KERNEL - 1:SC_KERNEL, :] * xs
        for d in range(1, SC_KERNEL):
            kk = SC_KERNEL - 1 - d
            acc = acc + wsc_ref[kk:kk + 1, :] * pltpu.roll(xs, d, axis=0)
        sc_buf[r0:r0 + rc, :] = acc[HALO_SC:, :]
    ch_ext[0:HALO_SC, :] = ch_ext[tm:tm + HALO_SC, :]
    y3_ref[0, :, D_GRP:2 * D_GRP] = (z[:, C_SCB:C_SCB + D_GRP] * sc_buf[...]).astype(BF16)

    pv = z[:, C_POOL:C_POOL + D_GRP]
    pv_ext[HALO_P:HALO_P + tm, :] = pv
    lane = lax.broadcasted_iota(jnp.int32, (1, D_GRP), 1)
    grp = lane // (D_GRP // len(POOL_WINDOWS))
    for r0 in range(0, tm, rc):
        run = pv_ext[r0:r0 + rc + HALO_P, :]
        sel = None
        w = 1
        for gi, wg in enumerate(POOL_WINDOWS):
            while w < wg:
                run = run + pltpu.roll(run, w, axis=0)
                w *= 2
            sel = run if sel is None else jnp.where(grp >= gi, run, sel)
        pool_buf[r0:r0 + rc, :] = sel[HALO_P:, :]
    pv_ext[0:HALO_P, :] = pv_ext[tm:tm + HALO_P, :]
    wl = jnp.zeros((1, D_GRP), jnp.int32)
    for gi, wg in enumerate(POOL_WINDOWS):
        wl = jnp.where(grp == gi, wg, wl)
    pos = s * tm + lax.broadcasted_iota(jnp.int32, (tm, D_GRP), 0)
    cnt = jnp.minimum(pos + 1, wl).astype(F32)
    d = pool_buf[...] / cnt - pv
    yp = _dot(d.astype(BF16), wpool_ref[...]) * pscale_ref[...]
    y3_ref[0, :, 2 * D_GRP:3 * D_GRP] = yp.astype(BF16)


def _layer_spec(a, layer, **kw):
    nd = a.ndim - 1
    return pl.BlockSpec((None,) + a.shape[1:], lambda *_: (layer,) + (0,) * nd, **kw)


def _front(h, layer, g, win, bf, wdw, lng, lnb, wpw, wsc, wpool, pscale, e2):
    B, S, _ = h.shape
    tm = min(TM_FRONT, S)
    nb = tm // STAT_ROWS
    grid = (B, S // tm)
    tok = lambda w: pl.BlockSpec((1, tm, w), lambda b, s: (b, s, 0))
    out_shape = (
        jax.ShapeDtypeStruct((B, S, D_GRP), BF16),
        jax.ShapeDtypeStruct((B, S, D_GRP), BF16),
        jax.ShapeDtypeStruct((B, D_GRP, S), BF16),
        jax.ShapeDtypeStruct((B, S, 3 * D_GRP), BF16),
        jax.ShapeDtypeStruct((B, S, LANES), F32),
        jax.ShapeDtypeStruct((B, SUBLANES, S), F32),
        jax.ShapeDtypeStruct((B, S // tm, 3 * nb, LANES), F32),
    )
    out_specs = (
        tok(D_GRP), tok(D_GRP),
        pl.BlockSpec((1, D_GRP, tm), lambda b, s: (b, 0, s)),
        tok(3 * D_GRP), tok(LANES),
        pl.BlockSpec((1, SUBLANES, tm), lambda b, s: (b, 0, s)),
        pl.BlockSpec((1, 1, 3 * nb, LANES), lambda b, s: (b, s, 0, 0)),
    )
    stacked = (g, win, bf, wdw, lng, lnb, wpw, wsc, wpool, pscale)
    consts = stacked + (e2,)
    in_specs = ([tok(D_MODEL)] + [_layer_spec(a, layer) for a in stacked]
                + [pl.BlockSpec(e2.shape, lambda b, s: (0, 0))])
    return pl.pallas_call(
        functools.partial(_front_kernel, tm=tm),
        grid=grid, in_specs=in_specs, out_specs=out_specs, out_shape=out_shape,
        scratch_shapes=[
            pltpu.VMEM((HALO_U + tm, D_GRP), F32),
            pltpu.VMEM((HALO_SC + tm, D_GRP), F32),
            pltpu.VMEM((HALO_P + tm, D_GRP), F32),
            pltpu.VMEM((tm, D_GRP), F32),
            pltpu.VMEM((tm, D_GRP), F32),
            pltpu.VMEM((tm, D_GRP), F32),
            pltpu.VMEM((SUBLANES, LANES), F32),
        ],
        compiler_params=pltpu.CompilerParams(
            dimension_semantics=("arbitrary", "arbitrary"),
            vmem_limit_bytes=VMEM_LIMIT),
        name="front",
    )(h, *consts)


def _first_needed_block(bstat, tq):
    B, nt, nb3, _ = bstat.shape
    nb = nb3 // 3
    flat = lambda a: a.reshape(B, nt * nb, LANES)
    n2 = flat(bstat[:, :, 0:nb])
    cs = flat(bstat[:, :, nb:2 * nb])[..., 0:N_HEADS]
    ce = flat(bstat[:, :, 2 * nb:3 * nb])[..., 0:N_HEADS]
    r = tq // STAT_ROWS
    nq = nt * nb // r
    qn = jnp.sqrt(jnp.max(n2[..., 0:N_HEADS].reshape(B, nq, r, N_HEADS), axis=2))
    kn = jnp.sqrt(jnp.max(n2[..., N_HEADS:2 * N_HEADS], axis=1, keepdims=True))
    bound = 2.0 * NORM_SLACK * qn * kn + 1.0
    c_start = cs[:, ::r]
    c_end = ce[:, r - 1::r]
    gap = c_start[:, :, None, :] - c_end[:, None, :, :]
    ii = jnp.arange(nq)[:, None]
    jj = jnp.arange(nq)[None, :]
    skip = (gap + bound[:, :, None, :] <= -SKIP_LOG2) & (jj < ii)[None, :, :, None]
    first = jnp.min(jnp.where(skip, nq, jj[None, :, :, None]), axis=2)
    return jnp.min(first, axis=2).astype(jnp.int32)


def _attn_kernel(jlo_ref, q_ref, k_ref, vt_ref, ccol_ref, crow_ref, o_ref, pb_s, *, tq):
    b = pl.program_id(0)
    n = q_ref.shape[1] // tq
    first = pl.program_id(1) * n

    def body(t, carry):
        _attn_query_block(b, first + t, t, jlo_ref, q_ref, k_ref, vt_ref, ccol_ref,
                          crow_ref, o_ref, pb_s, tq)
        return carry

    lax.fori_loop(0, n, body, 0)


def _attn_query_block(b, qi, t, jlo_ref, q_ref, k_ref, vt_ref, ccol_ref, crow_ref, o_ref,
                      pb_s, tq):
    qs = pl.multiple_of(qi * tq, tq)
    q_rows = pl.ds(pl.multiple_of(t * tq, tq), tq)
    lane = lax.broadcasted_iota(jnp.int32, (1, LANES), 1)
    nt = (((1,), (1,)), ((), ()))
    hpp = LANES // HEAD_DIM
    n_tiles = D_GRP // LANES
    hcols = lambda a, head: a[:, head * tq:(head + 1) * tq]
    hrows = lambda a, head: a[head * HEAD_DIM:(head + 1) * HEAD_DIM, :]

    q_tiles = []
    for t in range(n_tiles):
        q_pair = q_ref[0, q_rows, t * LANES:(t + 1) * LANES]
        q_tiles.append(jnp.concatenate(
            [jnp.where((lane // HEAD_DIM) == hh, q_pair, jnp.zeros_like(q_pair))
             for hh in range(hpp)], axis=0))
    c_refs = [crow_ref[0, head:head + 1, pl.ds(qs, LANES)][:, 0:1]
              for head in range(N_HEADS)]

    def scores(starts, prev_live=None):
        parts = []
        for t in range(n_tiles):
            kb = jnp.concatenate([k_ref[0, pl.ds(ks, tq), t * LANES:(t + 1) * LANES]
                                  for ks in starts], axis=0)
            s2 = lax.dot_general(kb, q_tiles[t], nt, preferred_element_type=F32)
            for hh in range(hpp):
                head = t * hpp + hh
                cks = [ccol_ref[0, pl.ds(ks, tq), head:head + 1] - c_refs[head]
                       for ks in starts]
                if prev_live is not None:
                    cks[0] = jnp.where(prev_live, cks[0], jnp.inf)
                parts.append(hcols(s2, hh) - jnp.concatenate(cks, axis=0))
        return jnp.concatenate(parts, axis=1)

    def weighted_values(pb, starts):
        outs = []
        for head in range(N_HEADS):
            vtb = jnp.concatenate(
                [vt_ref[0, head * HEAD_DIM:(head + 1) * HEAD_DIM, pl.ds(ks, tq)]
                 for ks in starts], axis=1)
            outs.append(_dot(vtb, hcols(pb, head)))
        return outs

    prev_start = pl.multiple_of(jnp.maximum(qi - 1, 0) * tq, tq)
    starts = [prev_start, qs]
    ch = ATTN_ROWS
    n_ch = 2 * tq // ch
    row_i = lax.broadcasted_iota(jnp.int32, (ch, LANES), 0)
    lane_i = lax.broadcasted_iota(jnp.int32, (ch, LANES), 1)
    fold = lambda a: a.reshape(ch // SUBLANES, SUBLANES, LANES)
    m_parts, l_parts = [], []
    for t in range(n_tiles):
        kb = jnp.concatenate([k_ref[0, pl.ds(ks, tq), t * LANES:(t + 1) * LANES]
                              for ks in starts], axis=0)
        s2 = lax.dot_general(kb, q_tiles[t], nt, preferred_element_type=F32)
        for hh in range(hpp):
            head = t * hpp + hh
            ck_prev = ccol_ref[0, pl.ds(prev_start, tq), head:head + 1] - c_refs[head]
            ck_prev = jnp.where(qi > 0, ck_prev, jnp.inf)
            ck_diag = ccol_ref[0, pl.ds(qs, tq), head:head + 1] - c_refs[head]
            ck = jnp.concatenate([ck_prev, ck_diag], axis=0)
            n_j = tq // LANES

            def chunk(c, j, ckb, hh=hh, s2=s2):
                c0 = hh * tq + j * LANES
                sc = s2[c * ch:(c + 1) * ch, c0:c0 + LANES] - ckb
                k0 = c * ch - tq
                if k0 + ch - 1 <= j * LANES:
                    return sc
                if k0 > j * LANES + LANES - 1:
                    return None
                return jnp.where(row_i + (k0 - j * LANES) <= lane_i, sc, -jnp.inf)

            mx = [None] * n_j
            for c in range(n_ch):
                ckb = jnp.broadcast_to(ck[c * ch:(c + 1) * ch], (ch, LANES))
                for j in range(n_j):
                    sc = chunk(c, j, ckb)
                    if sc is not None:
                        v = jnp.max(fold(sc), axis=0)
                        mx[j] = v if mx[j] is None else jnp.maximum(mx[j], v)
            m_t = [jnp.max(v, axis=0, keepdims=True) for v in mx]
            ls = [None] * n_j
            for c in range(n_ch):
                ckb = jnp.broadcast_to(ck[c * ch:(c + 1) * ch], (ch, LANES))
                rows = slice(c * ch, (c + 1) * ch)
                for j in range(n_j):
                    o0 = head * tq + j * LANES
                    sc = chunk(c, j, ckb)
                    if sc is None:
                        pb_s[rows, o0:o0 + LANES] = jnp.zeros((ch, LANES), BF16)
                        continue
                    p = jnp.exp2(sc - m_t[j])
                    v = jnp.sum(fold(p), axis=0)
                    ls[j] = v if ls[j] is None else ls[j] + v
                    pb_s[rows, o0:o0 + LANES] = p.astype(BF16)
            m_parts += m_t
            l_parts += [jnp.sum(v, axis=0, keepdims=True) for v in ls]
    m = jnp.concatenate(m_parts, axis=1)
    l = jnp.concatenate(l_parts, axis=1)
    acc = jnp.concatenate(weighted_values(pb_s[...], starts), axis=0)

    def block(j, carry):
        m, l, acc = carry
        ks = pl.multiple_of(j * tq, tq)
        st = scores([ks])
        m_new = jnp.maximum(m, jnp.max(st, axis=0, keepdims=True))
        alpha = jnp.exp2(m - m_new)
        p = jnp.exp2(st - m_new)
        l = alpha * l + jnp.sum(p, axis=0, keepdims=True)
        pv = weighted_values(p.astype(BF16), [ks])
        accs = [hcols(alpha, head) * hrows(acc, head) + pv[head]
                for head in range(N_HEADS)]
        return m_new, l, jnp.concatenate(accs, axis=0)

    _, l, acc = lax.fori_loop(jlo_ref[b, qi], qi - 1, block, (m, l, acc))
    o_t = jnp.concatenate([hrows(acc, head) / hcols(l, head)
                           for head in range(N_HEADS)], axis=0)
    o_ref[0, q_rows, :] = o_t.T.astype(BF16)


def _attention(jlo, q, k, vt, ccol, crow):
    B, S, _ = q.shape
    tq = min(TQ, S)
    ts = min(TQ * ATTN_BLOCKS_PER_STEP, S)
    grid_spec = pltpu.PrefetchScalarGridSpec(
        num_scalar_prefetch=1,
        grid=(B, S // ts),
        in_specs=[
            pl.BlockSpec((1, ts, D_GRP), lambda b, i, jlo: (b, i, 0)),
            pl.BlockSpec((1, S, D_GRP), lambda b, i, jlo: (b, 0, 0)),
            pl.BlockSpec((1, D_GRP, S), lambda b, i, jlo: (b, 0, 0)),
            pl.BlockSpec((1, S, LANES), lambda b, i, jlo: (b, 0, 0)),
            pl.BlockSpec((1, SUBLANES, S), lambda b, i, jlo: (b, 0, 0)),
        ],
        out_specs=pl.BlockSpec((1, ts, D_GRP), lambda b, i, jlo: (b, i, 0)),
        scratch_shapes=[pltpu.VMEM((2 * tq, N_HEADS * tq), BF16)],
    )
    return pl.pallas_call(
        functools.partial(_attn_kernel, tq=tq),
        grid_spec=grid_spec,
        out_shape=jax.ShapeDtypeStruct((B, S, D_GRP), BF16),
        compiler_params=pltpu.CompilerParams(
            dimension_semantics=("arbitrary", "arbitrary"),
            vmem_limit_bytes=VMEM_LIMIT),
        name="attn",
    )(jlo, q, k, vt, ccol, crow)


def _back_kernel(h_ref, y3_ref, ya_ref, p_ref, gains_ref, wo_ref,
                 wup_ref, wdn_ref, wgate_ref, wproj_ref, o_ref):
    g_mix_post = gains_ref[0:1, :]
    g_mlp_pre = gains_ref[1:2, :]
    g_mlp_post = gains_ref[2:3, :]
    g_ple_pre = gains_ref[3:4, :]
    g_ple_post = gains_ref[4:5, :]

    h = h_ref[0]
    y = jnp.concatenate([y3_ref[0], ya_ref[0]], axis=1)
    h = h + _rms(_dot(y, wo_ref[...]), g_mix_post)

    r = _inv_rms(h)
    up = jnp.maximum(_dot((h * g_mlp_pre).astype(BF16), wup_ref[...]), 0.0)
    f = _dot((up * up).astype(BF16), wdn_ref[...])
    r2 = r * r
    f_scale = r2 * lax.rsqrt(r2 * r2 * jnp.mean(f * f, axis=-1, keepdims=True) + EPS)
    h = h + f * f_scale * g_mlp_post

    r = _inv_rms(h)
    gate = _sigmoid(r * _dot((h * g_ple_pre).astype(BF16), wgate_ref[...]))
    e = _dot(p_ref[0, 0].astype(BF16), wproj_ref[...]) * gate
    o_ref[0] = h + _rms(e, g_ple_post)


def _back(h, y3, ya, p, layer, gains, wo, wup, wdn, wgate, wproj):
    B, S, _ = h.shape
    tm = min(TM_BACK, S)
    grid = (B, S // tm)
    tok = lambda w: pl.BlockSpec((1, tm, w), lambda b, s: (b, s, 0))
    weights = (gains, wo, wup, wdn, wgate, wproj)
    in_specs = [
        tok(D_MODEL), tok(3 * D_GRP), tok(D_GRP),
        pl.BlockSpec((1, 1, tm, D_PLE), lambda b, s: (layer, b, s, 0)),
    ] + [_layer_spec(a, layer, pipeline_mode=pl.Buffered(1)) for a in weights]
    return pl.pallas_call(
        _back_kernel,
        grid=grid, in_specs=in_specs, out_specs=tok(D_MODEL),
        out_shape=jax.ShapeDtypeStruct((B, S, D_MODEL), F32),
        compiler_params=pltpu.CompilerParams(
            dimension_semantics=("arbitrary", "arbitrary"),
            vmem_limit_bytes=BACK_VMEM_LIMIT),
        name="back",
    )(h, y3, ya, p, *weights)


def _permute_w_in(w_in):
    o = 0
    conf = w_in[..., o:o + 512]; o += 512
    q = w_in[..., o:o + 256] * (HEAD_DIM ** -0.5 * LOG2E); o += 256
    kv = w_in[..., o:o + 512]; o += 512
    f = w_in[..., o:o + N_HEADS]; o += N_HEADS
    rest = w_in[..., o:]
    f = jnp.pad(f, ((0, 0), (0, 0), (0, LANES - N_HEADS)))
    return jnp.concatenate([a.astype(BF16) for a in (conf, q, kv, rest, f)], axis=-1)


def _block_diag(w_pool):
    L, G, C, _ = w_pool.shape
    eye = jnp.eye(G, dtype=w_pool.dtype)
    return jnp.einsum("lgcd,gh->lgchd", w_pool, eye).reshape(L, G * C, G * C)


def _head_indicator():
    e = np.zeros((2 * D_GRP, LANES), np.float32)
    for d in range(2 * D_GRP):
        e[d, d // HEAD_DIM] = 1.0
    return jnp.asarray(e, BF16)


def kernel(x, p, g_mix_pre, w_in, b_forget, w_conf_dw, conf_ln_g, conf_ln_b, w_conf_pw, w_sc, w_pool, pool_scale, w_out, g_mix_post, g_mlp_pre, w_up, w_down, g_mlp_post, g_ple_pre, w_ple_gate, w_ple_proj, g_ple_post):
    L = w_in.shape[0]
    S = x.shape[1]
    win = _permute_w_in(w_in)
    bf = jnp.broadcast_to(jnp.pad(b_forget, ((0, 0), (0, SUBLANES - N_HEADS)))[:, :, None],
                          (L, SUBLANES, LANES))
    wpw = w_conf_pw.astype(BF16)
    wpool = _block_diag(w_pool).astype(BF16)
    wo = jnp.concatenate([w_out[:, 0:256], w_out[:, 512:1024], w_out[:, 256:512]],
                         axis=1).astype(BF16)
    wup = w_up.astype(BF16)
    wdn = w_down.astype(BF16)
    wgate = w_ple_gate.astype(BF16)
    wproj = w_ple_proj.astype(BF16)
    zero = jnp.zeros_like(g_mix_post)
    gains = jnp.stack([g_mix_post, g_mlp_pre, g_mlp_post, g_ple_pre, g_ple_post,
                       zero, zero, zero], axis=1)
    e2 = _head_indicator()

    row = lambda a: a[:, None, :]
    h = x
    for i in range(L):
        q, k, vt, y3, ccol, crow, bstat = _front(
            h, i, row(g_mix_pre), win, bf, w_conf_dw, row(conf_ln_g), row(conf_ln_b),
            wpw, w_sc, wpool, row(pool_scale), e2)
        jlo = _first_needed_block(bstat, min(TQ, S))
        ya = _attention(jlo, q, k, vt, ccol, crow)
        h = _back(h, y3, ya, p, i, gains, wo, wup, wdn, wgate, wproj)
    return h
```

```python
import functools

import jax
import jax.numpy as jnp
import numpy as np
from jax import lax
from jax.experimental import pallas as pl
from jax.experimental.pallas import tpu as pltpu

F32 = jnp.float32
BF16 = jnp.bfloat16

D_MODEL = 1024
D_GRP = 256
HEAD_DIM = 64
N_HEADS = D_GRP // HEAD_DIM
CONF_KERNEL = 31
SC_KERNEL = 3
POOL_WINDOWS = (2, 4, 8, 16)
D_FF = 4 * D_MODEL
D_PLE = 256
EPS = 1e-6
LANES = 128
SUBLANES = 8
LOG2E = 1.4426950408889634

C_CONF = 0
C_Q = 512
C_K = 768
C_V = 1024
C_SCH = 1280
C_SCB = 1536
C_SCC = 1792
C_POOL = 2048
C_F = 2304
D_Z = C_F + LANES

HALO_U = 32
HALO_SC = 8
HALO_P = 16
ROW_CHUNK = 64
CONV_ROWS = 64
STAT_ROWS = 128

TM_FRONT = 1024
TM_BACK = 1024
TQ = 256
ATTN_BLOCKS_PER_STEP = 4
ONES_ROWS = 16
VMEM_LIMIT = 56 * 1024 * 1024
BACK_VMEM_LIMIT = 60 * 1024 * 1024

SKIP_LOG2 = 104.0 * LOG2E + 1.0
NORM_SLACK = 1.02


def _inv_rms(x):
    return lax.rsqrt(jnp.mean(x * x, axis=-1, keepdims=True) + EPS)


def _rms(x, g):
    return x * _inv_rms(x) * g


def _sigmoid(x):
    return 1.0 / (1.0 + jnp.exp(-x))


def _dot(a, b):
    return jnp.dot(a, b, preferred_element_type=F32)


def _front_kernel(h_ref, g_ref, win_ref, bf_ref, wdw_ref, lng_ref, lnb_ref,
                  wpw_ref, wsc_ref, wpool_ref, pscale_ref, e2_ref,
                  q_ref, k_ref, vt_ref, y3_ref, ccol_ref, crow_ref, bstat_ref,
                  u_ext, ch_ext, pv_ext, conv_buf, sc_buf, pool_buf, c_carry, *, tm):
    s = pl.program_id(1)

    @pl.when(s == 0)
    def _():
        u_ext[0:HALO_U, :] = jnp.zeros((HALO_U, D_GRP), F32)
        ch_ext[0:HALO_SC, :] = jnp.zeros((HALO_SC, D_GRP), F32)
        pv_ext[0:HALO_P, :] = jnp.zeros((HALO_P, D_GRP), F32)
        c_carry[...] = jnp.zeros((SUBLANES, LANES), F32)

    h = h_ref[0]
    xn = _rms(h, g_ref[...]).astype(BF16)
    z = _dot(xn, win_ref[...])

    qf = z[:, C_Q:C_Q + D_GRP]
    kf = z[:, C_K:C_K + D_GRP]
    q_ref[0] = qf.astype(BF16)
    k_ref[0] = kf.astype(BF16)
    vt_ref[0] = z[:, C_V:C_V + D_GRP].T.astype(BF16)

    sq = jnp.concatenate([qf * qf, kf * kf], axis=1).astype(BF16)
    n2 = _dot(sq, e2_ref[...])

    f = z[:, C_F:C_F + LANES].T[0:SUBLANES, :] + bf_ref[:, 0:1]
    logf = (jnp.minimum(f, 0.0) - jnp.log(1.0 + jnp.exp(-jnp.abs(f)))) * LOG2E
    pos_l = lax.broadcasted_iota(jnp.int32, (SUBLANES, tm), 1)
    ct = logf
    shift = 1
    while shift < tm:
        ct = ct + jnp.where(pos_l >= shift, pltpu.roll(ct, shift, axis=1), 0.0)
        shift *= 2
    ct = ct + c_carry[:, 0:1]
    c_carry[...] = jnp.broadcast_to(ct[:, tm - 1:tm], (SUBLANES, LANES))
    crow_ref[0] = ct
    c = jnp.concatenate([ct, jnp.zeros((LANES - SUBLANES, tm), F32)], axis=0).T
    ccol_ref[0] = c

    nb = tm // STAT_ROWS
    c3 = c.reshape(nb, STAT_ROWS, LANES)
    bstat_ref[0, 0, 0:nb, :] = jnp.max(n2.reshape(nb, STAT_ROWS, LANES), axis=1)
    bstat_ref[0, 0, nb:2 * nb, :] = c3[:, 0, :]
    bstat_ref[0, 0, 2 * nb:3 * nb, :] = c3[:, STAT_ROWS - 1, :]

    u = z[:, C_CONF:C_CONF + D_GRP] * _sigmoid(z[:, C_CONF + D_GRP:C_CONF + 2 * D_GRP])
    u_ext[HALO_U:HALO_U + tm, :] = u
    n_a = (CONF_KERNEL - 1) // SUBLANES + 1
    rc = ROW_CHUNK
    for r0, l0 in [(r0, l0) for r0 in range(0, tm, CONV_ROWS)
                   for l0 in range(0, D_GRP, LANES)]:
        base = HALO_U + r0 - SUBLANES
        ls = slice(l0, l0 + LANES)
        ua = [u_ext[base - SUBLANES * a:base - SUBLANES * a + CONV_ROWS + SUBLANES, ls]
              for a in range(n_a)]
        total = None
        for r in range(SUBLANES):
            part = None
            for a in range(n_a):
                d = SUBLANES * a + r
                if d > CONF_KERNEL - 1:
                    continue
                kk = CONF_KERNEL - 1 - d
                term = wdw_ref[kk:kk + 1, ls] * ua[a]
                part = term if part is None else part + term
            if r:
                part = pltpu.roll(part, r, axis=0)
            part = part[SUBLANES:, :]
            total = part if total is None else total + part
        conv_buf[r0:r0 + CONV_ROWS, ls] = total
    u_ext[0:HALO_U, :] = u_ext[tm:tm + HALO_U, :]
    cv = conv_buf[...]
    mu = jnp.mean(cv, axis=-1, keepdims=True)
    xc = cv - mu
    ln = xc * lax.rsqrt(jnp.mean(xc * xc, axis=-1, keepdims=True) + EPS)
    ln = ln * lng_ref[...] + lnb_ref[...]
    act = ln * _sigmoid(ln)
    y3_ref[0, :, 0:D_GRP] = _dot(act.astype(BF16), wpw_ref[...]).astype(BF16)

    ch_ext[HALO_SC:HALO_SC + tm, :] = z[:, C_SCC:C_SCC + D_GRP] * z[:, C_SCH:C_SCH + D_GRP]
    for r0 in range(0, tm, rc):
        xs = ch_ext[r0:r0 + rc + HALO_SC, :]
        acc = wsc_ref[SC_KERNEL - 1:SC_KERNEL, :] * xs
        for d in range(1, SC_KERNEL):
            kk = SC_KERNEL - 1 - d
            acc = acc + wsc_ref[kk:kk + 1, :] * pltpu.roll(xs, d, axis=0)
        sc_buf[r0:r0 + rc, :] = acc[HALO_SC:, :]
    ch_ext[0:HALO_SC, :] = ch_ext[tm:tm + HALO_SC, :]
    y3_ref[0, :, D_GRP:2 * D_GRP] = (z[:, C_SCB:C_SCB + D_GRP] * sc_buf[...]).astype(BF16)

    pv = z[:, C_POOL:C_POOL + D_GRP]
    pv_ext[HALO_P:HALO_P + tm, :] = pv
    lane = lax.broadcasted_iota(jnp.int32, (1, D_GRP), 1)
    grp = lane // (D_GRP // len(POOL_WINDOWS))
    for r0 in range(0, tm, rc):
        run = pv_ext[r0:r0 + rc + HALO_P, :]
        sel = None
        w = 1
        for gi, wg in enumerate(POOL_WINDOWS):
            while w < wg:
                run = run + pltpu.roll(run, w, axis=0)
                w *= 2
            sel = run if sel is None else jnp.where(grp >= gi, run, sel)
        pool_buf[r0:r0 + rc, :] = sel[HALO_P:, :]
    pv_ext[0:HALO_P, :] = pv_ext[tm:tm + HALO_P, :]
    wl = jnp.zeros((1, D_GRP), jnp.int32)
    for gi, wg in enumerate(POOL_WINDOWS):
        wl = jnp.where(grp == gi, wg, wl)
    pos = s * tm + lax.broadcasted_iota(jnp.int32, (tm, D_GRP), 0)
    cnt = jnp.minimum(pos + 1, wl).astype(F32)
    d = pool_buf[...] / cnt - pv
    yp = _dot(d.astype(BF16), wpool_ref[...]) * pscale_ref[...]
    y3_ref[0, :, 2 * D_GRP:3 * D_GRP] = yp.astype(BF16)


def _layer_spec(a, layer, **kw):
    nd = a.ndim - 1
    return pl.BlockSpec((None,) + a.shape[1:], lambda *_: (layer,) + (0,) * nd, **kw)


def _front(h, layer, g, win, bf, wdw, lng, lnb, wpw, wsc, wpool, pscale, e2):
    B, S, _ = h.shape
    tm = min(TM_FRONT, S)
    nb = tm // STAT_ROWS
    grid = (B, S // tm)
    tok = lambda w: pl.BlockSpec((1, tm, w), lambda b, s: (b, s, 0))
    out_shape = (
        jax.ShapeDtypeStruct((B, S, D_GRP), BF16),
        jax.ShapeDtypeStruct((B, S, D_GRP), BF16),
        jax.ShapeDtypeStruct((B, D_GRP, S), BF16),
        jax.ShapeDtypeStruct((B, S, 3 * D_GRP), BF16),
        jax.ShapeDtypeStruct((B, S, LANES), F32),
        jax.ShapeDtypeStruct((B, SUBLANES, S), F32),
        jax.ShapeDtypeStruct((B, S // tm, 3 * nb, LANES), F32),
    )
    out_specs = (
        tok(D_GRP), tok(D_GRP),
        pl.BlockSpec((1, D_GRP, tm), lambda b, s: (b, 0, s)),
        tok(3 * D_GRP), tok(LANES),
        pl.BlockSpec((1, SUBLANES, tm), lambda b, s: (b, 0, s)),
        pl.BlockSpec((1, 1, 3 * nb, LANES), lambda b, s: (b, s, 0, 0)),
    )
    stacked = (g, win, bf, wdw, lng, lnb, wpw, wsc, wpool, pscale)
    consts = stacked + (e2,)
    in_specs = ([tok(D_MODEL)] + [_layer_spec(a, layer) for a in stacked]
                + [pl.BlockSpec(e2.shape, lambda b, s: (0, 0))])
    return pl.pallas_call(
        functools.partial(_front_kernel, tm=tm),
        grid=grid, in_specs=in_specs, out_specs=out_specs, out_shape=out_shape,
        scratch_shapes=[
            pltpu.VMEM((HALO_U + tm, D_GRP), F32),
            pltpu.VMEM((HALO_SC + tm, D_GRP), F32),
            pltpu.VMEM((HALO_P + tm, D_GRP), F32),
            pltpu.VMEM((tm, D_GRP), F32),
            pltpu.VMEM((tm, D_GRP), F32),
            pltpu.VMEM((tm, D_GRP), F32),
            pltpu.VMEM((SUBLANES, LANES), F32),
        ],
        compiler_params=pltpu.CompilerParams(
            dimension_semantics=("arbitrary", "arbitrary"),
            vmem_limit_bytes=VMEM_LIMIT),
        name="front",
    )(h, *consts)


def _first_needed_block(bstat, tq):
    B, nt, nb3, _ = bstat.shape
    nb = nb3 // 3
    flat = lambda a: a.reshape(B, nt * nb, LANES)
    n2 = flat(bstat[:, :, 0:nb])
    cs = flat(bstat[:, :, nb:2 * nb])[..., 0:N_HEADS]
    ce = flat(bstat[:, :, 2 * nb:3 * nb])[..., 0:N_HEADS]
    r = tq // STAT_ROWS
    nq = nt * nb // r
    qn = jnp.sqrt(jnp.max(n2[..., 0:N_HEADS].reshape(B, nq, r, N_HEADS), axis=2))
    kn = jnp.sqrt(jnp.max(n2[..., N_HEADS:2 * N_HEADS], axis=1, keepdims=True))
    bound = 2.0 * NORM_SLACK * qn * kn + 1.0
    c_start = cs[:, ::r]
    c_end = ce[:, r - 1::r]
    gap = c_start[:, :, None, :] - c_end[:, None, :, :]
    ii = jnp.arange(nq)[:, None]
    jj = jnp.arange(nq)[None, :]
    skip = (gap + bound[:, :, None, :] <= -SKIP_LOG2) & (jj < ii)[None, :, :, None]
    first = jnp.min(jnp.where(skip, nq, jj[None, :, :, None]), axis=2)
    return jnp.min(first, axis=2).astype(jnp.int32)


def _attn_kernel(jlo_ref, q_ref, k_ref, vt_ref, ccol_ref, crow_ref, o_ref, *, tq):
    b = pl.program_id(0)
    n = q_ref.shape[1] // tq
    first = pl.program_id(1) * n

    def body(t, carry):
        _attn_query_block(b, first + t, t, jlo_ref, q_ref, k_ref, vt_ref, ccol_ref,
                          crow_ref, o_ref, tq)
        return carry

    lax.fori_loop(0, n, body, 0)


def _attn_query_block(b, qi, t, jlo_ref, q_ref, k_ref, vt_ref, ccol_ref, crow_ref, o_ref, tq):
    qs = pl.multiple_of(qi * tq, tq)
    q_rows = pl.ds(pl.multiple_of(t * tq, tq), tq)
    lane = lax.broadcasted_iota(jnp.int32, (1, LANES), 1)
    nt = (((1,), (1,)), ((), ()))
    hpp = LANES // HEAD_DIM
    n_tiles = D_GRP // LANES
    hcols = lambda a, head: a[:, head * tq:(head + 1) * tq]
    hrows = lambda a, head: a[head * HEAD_DIM:(head + 1) * HEAD_DIM, :]

    q_tiles = []
    for t in range(n_tiles):
        q_pair = q_ref[0, q_rows, t * LANES:(t + 1) * LANES]
        q_tiles.append(jnp.concatenate(
            [jnp.where((lane // HEAD_DIM) == hh, q_pair, jnp.zeros_like(q_pair))
             for hh in range(hpp)], axis=0))
    c_refs = [crow_ref[0, head:head + 1, pl.ds(qs, LANES)][:, 0:1]
              for head in range(N_HEADS)]

    def scores(starts, prev_live=None):
        parts = []
        for t in range(n_tiles):
            kb = jnp.concatenate([k_ref[0, pl.ds(ks, tq), t * LANES:(t + 1) * LANES]
                                  for ks in starts], axis=0)
            s2 = lax.dot_general(kb, q_tiles[t], nt, preferred_element_type=F32)
            for hh in range(hpp):
                head = t * hpp + hh
                cks = [ccol_ref[0, pl.ds(ks, tq), head:head + 1] - c_refs[head]
                       for ks in starts]
                if prev_live is not None:
                    cks[0] = jnp.where(prev_live, cks[0], jnp.inf)
                parts.append(hcols(s2, hh) - jnp.concatenate(cks, axis=0))
        return jnp.concatenate(parts, axis=1)

    def weighted_values(pb, starts):
        n_keys = len(starts) * tq
        ones = jnp.ones((ONES_ROWS, n_keys), BF16)
        outs = []
        for head in range(N_HEADS):
            vtb = jnp.concatenate(
                [vt_ref[0, head * HEAD_DIM:(head + 1) * HEAD_DIM, pl.ds(ks, tq)]
                 for ks in starts], axis=1)
            outs.append(_dot(jnp.concatenate([vtb, ones], axis=0), hcols(pb, head)))
        return (jnp.concatenate([o[0:HEAD_DIM] for o in outs], axis=0),
                jnp.concatenate([o[HEAD_DIM:HEAD_DIM + 1] for o in outs], axis=1))

    prev_start = pl.multiple_of(jnp.maximum(qi - 1, 0) * tq, tq)
    st = scores([prev_start, qs], prev_live=qi > 0)
    key_i = lax.broadcasted_iota(jnp.int32, (tq, N_HEADS * tq), 0)
    qry_i = lax.broadcasted_iota(jnp.int32, (tq, N_HEADS * tq), 1) % tq
    st = jnp.concatenate([st[0:tq], jnp.where(key_i <= qry_i, st[tq:], -jnp.inf)], axis=0)
    m = jnp.max(st, axis=0, keepdims=True)
    acc, l = weighted_values(jnp.exp2(st - m).astype(BF16), [prev_start, qs])

    def block(j, carry):
        m, l, acc = carry
        ks = pl.multiple_of(j * tq, tq)
        st = scores([ks])
        m_new = jnp.maximum(m, jnp.max(st, axis=0, keepdims=True))
        alpha = jnp.exp2(m - m_new)
        pv, ps = weighted_values(jnp.exp2(st - m_new).astype(BF16), [ks])
        accs = [hcols(alpha, head) * hrows(acc, head) + hrows(pv, head)
                for head in range(N_HEADS)]
        return m_new, alpha * l + ps, jnp.concatenate(accs, axis=0)

    _, l, acc = lax.fori_loop(jlo_ref[b, qi], qi - 1, block, (m, l, acc))
    o_t = jnp.concatenate([hrows(acc, head) / hcols(l, head)
                           for head in range(N_HEADS)], axis=0)
    o_ref[0, q_rows, :] = o_t.T.astype(BF16)


def _attention(jlo, q, k, vt, ccol, crow):
    B, S, _ = q.shape
    tq = min(TQ, S)
    ts = min(TQ * ATTN_BLOCKS_PER_STEP, S)
    grid_spec = pltpu.PrefetchScalarGridSpec(
        num_scalar_prefetch=1,
        grid=(B, S // ts),
        in_specs=[
            pl.BlockSpec((1, ts, D_GRP), lambda b, i, jlo: (b, i, 0)),
            pl.BlockSpec((1, S, D_GRP), lambda b, i, jlo: (b, 0, 0)),
            pl.BlockSpec((1, D_GRP, S), lambda b, i, jlo: (b, 0, 0)),
            pl.BlockSpec((1, S, LANES), lambda b, i, jlo: (b, 0, 0)),
            pl.BlockSpec((1, SUBLANES, S), lambda b, i, jlo: (b, 0, 0)),
        ],
        out_specs=pl.BlockSpec((1, ts, D_GRP), lambda b, i, jlo: (b, i, 0)),
    )
    return pl.pallas_call(
        functools.partial(_attn_kernel, tq=tq),
        grid_spec=grid_spec,
        out_shape=jax.ShapeDtypeStruct((B, S, D_GRP), BF16),
        compiler_params=pltpu.CompilerParams(
            dimension_semantics=("arbitrary", "arbitrary"),
            vmem_limit_bytes=VMEM_LIMIT),
        name="attn",
    )(jlo, q, k, vt, ccol, crow)


def _back_kernel(h_ref, y3_ref, ya_ref, p_ref, gains_ref, wo_ref,
                 wup_ref, wdn_ref, wgate_ref, wproj_ref, o_ref):
    g_mix_post = gains_ref[0:1, :]
    g_mlp_pre = gains_ref[1:2, :]
    g_mlp_post = gains_ref[2:3, :]
    g_ple_pre = gains_ref[3:4, :]
    g_ple_post = gains_ref[4:5, :]

    h = h_ref[0]
    y = jnp.concatenate([y3_ref[0], ya_ref[0]], axis=1)
    h = h + _rms(_dot(y, wo_ref[...]), g_mix_post)

    r = _inv_rms(h)
    up = jnp.maximum(_dot((h * g_mlp_pre).astype(BF16), wup_ref[...]), 0.0)
    f = _dot((up * up).astype(BF16), wdn_ref[...])
    r2 = r * r
    f_scale = r2 * lax.rsqrt(r2 * r2 * jnp.mean(f * f, axis=-1, keepdims=True) + EPS)
    h = h + f * f_scale * g_mlp_post

    r = _inv_rms(h)
    gate = _sigmoid(r * _dot((h * g_ple_pre).astype(BF16), wgate_ref[...]))
    e = _dot(p_ref[0, 0].astype(BF16), wproj_ref[...]) * gate
    o_ref[0] = h + _rms(e, g_ple_post)


def _back(h, y3, ya, p, layer, gains, wo, wup, wdn, wgate, wproj):
    B, S, _ = h.shape
    tm = min(TM_BACK, S)
    grid = (B, S // tm)
    tok = lambda w: pl.BlockSpec((1, tm, w), lambda b, s: (b, s, 0))
    weights = (gains, wo, wup, wdn, wgate, wproj)
    in_specs = [
        tok(D_MODEL), tok(3 * D_GRP), tok(D_GRP),
        pl.BlockSpec((1, 1, tm, D_PLE), lambda b, s: (layer, b, s, 0)),
    ] + [_layer_spec(a, layer, pipeline_mode=pl.Buffered(1)) for a in weights]
    return pl.pallas_call(
        _back_kernel,
        grid=grid, in_specs=in_specs, out_specs=tok(D_MODEL),
        out_shape=jax.ShapeDtypeStruct((B, S, D_MODEL), F32),
        compiler_params=pltpu.CompilerParams(
            dimension_semantics=("arbitrary", "arbitrary"),
            vmem_limit_bytes=BACK_VMEM_LIMIT),
        name="back",
    )(h, y3, ya, p, *weights)


def _permute_w_in(w_in):
    o = 0
    conf = w_in[..., o:o + 512]; o += 512
    q = w_in[..., o:o + 256] * (HEAD_DIM ** -0.5 * LOG2E); o += 256
    kv = w_in[..., o:o + 512]; o += 512
    f = w_in[..., o:o + N_HEADS]; o += N_HEADS
    rest = w_in[..., o:]
    f = jnp.pad(f, ((0, 0), (0, 0), (0, LANES - N_HEADS)))
    return jnp.concatenate([a.astype(BF16) for a in (conf, q, kv, rest, f)], axis=-1)


def _block_diag(w_pool):
    L, G, C, _ = w_pool.shape
    eye = jnp.eye(G, dtype=w_pool.dtype)
    return jnp.einsum("lgcd,gh->lgchd", w_pool, eye).reshape(L, G * C, G * C)


def _head_indicator():
    e = np.zeros((2 * D_GRP, LANES), np.float32)
    for d in range(2 * D_GRP):
        e[d, d // HEAD_DIM] = 1.0
    return jnp.asarray(e, BF16)


def kernel(x, p, g_mix_pre, w_in, b_forget, w_conf_dw, conf_ln_g, conf_ln_b, w_conf_pw, w_sc, w_pool, pool_scale, w_out, g_mix_post, g_mlp_pre, w_up, w_down, g_mlp_post, g_ple_pre, w_ple_gate, w_ple_proj, g_ple_post):
    L = w_in.shape[0]
    S = x.shape[1]
    win = _permute_w_in(w_in)
    bf = jnp.broadcast_to(jnp.pad(b_forget, ((0, 0), (0, SUBLANES - N_HEADS)))[:, :, None],
                          (L, SUBLANES, LANES))
    wpw = w_conf_pw.astype(BF16)
    wpool = _block_diag(w_pool).astype(BF16)
    wo = jnp.concatenate([w_out[:, 0:256], w_out[:, 512:1024], w_out[:, 256:512]],
                         axis=1).astype(BF16)
    wup = w_up.astype(BF16)
    wdn = w_down.astype(BF16)
    wgate = w_ple_gate.astype(BF16)
    wproj = w_ple_proj.astype(BF16)
    zero = jnp.zeros_like(g_mix_post)
    gains = jnp.stack([g_mix_post, g_mlp_pre, g_mlp_post, g_ple_pre, g_ple_post,
                       zero, zero, zero], axis=1)
    e2 = _head_indicator()

    row = lambda a: a[:, None, :]
    h = x
    for i in range(L):
        q, k, vt, y3, ccol, crow, bstat = _front(
            h, i, row(g_mix_pre), win, bf, w_conf_dw, row(conf_ln_g), row(conf_ln_b),
            wpw, w_sc, wpool, row(pool_scale), e2)
        jlo = _first_needed_block(bstat, min(TQ, S))
        ya = _attention(jlo, q, k, vt, ccol, crow)
        h = _back(h, y3, ya, p, i, gains, wo, wup, wdn, wgate, wproj)
    return h
```
